```python
import jax, jax.numpy as jnp
from jax import lax
import numpy as np

D_MODEL = 1024
BATCH = 4
SEQ = 4096
DEPTH = 1

HEAD_DIM = 64
N_HEADS = D_MODEL // HEAD_DIM
DIL_HEADS = N_HEADS // 2
SB_HEADS = N_HEADS - DIL_HEADS
DIL_WIDTH = DIL_HEADS * HEAD_DIM
SB_WIDTH = SB_HEADS * HEAD_DIM
QKV_WIDTH = 3 * DIL_WIDTH + 3 * SB_WIDTH
DILATED_BRANCHES = ((128, 1), (512, 4), (2048, 16))
ROPE_DIM = HEAD_DIM // 4
ROPE_THETA = 500000.0
Q_BLOCK = 128
PEER_HEADS = 8
N_SUBKEYS = 128
N_EXPERTS = N_SUBKEYS * N_SUBKEYS
PEER_KEY_DIM = 256
PEER_TOPK = 16
TOKEN_BLOCK = 128
LN_EPS = 1e-5
DEEPNORM_ALPHA = (2.0 * DEPTH) ** 0.25
DEEPNORM_BETA = (8.0 * DEPTH) ** -0.25

kernel_name = "hybrid_dilated_stickbreak_peer_deepnorm"


def layer_norm(x, g, b):
    xf = x.astype(jnp.float32)
    mu = jnp.mean(xf, axis=-1, keepdims=True)
    var = jnp.mean(jnp.square(xf - mu), axis=-1, keepdims=True)
    y = (xf - mu) * lax.rsqrt(var + LN_EPS)
    return (y * g.astype(jnp.float32) + b.astype(jnp.float32)).astype(x.dtype)


def split_heads(t, n_heads):
    b, s, _ = t.shape
    return t.reshape(b, s, n_heads, HEAD_DIM).transpose(0, 2, 1, 3)


def merge_heads(t):
    b, h, s, d = t.shape
    return t.transpose(0, 2, 1, 3).reshape(b, s, h * d)


def partial_rotary(x, pos):
    half = ROPE_DIM // 2
    inv_freq = ROPE_THETA ** (-jnp.arange(half, dtype=jnp.float32) * 2.0 / ROPE_DIM)
    ang = pos.astype(jnp.float32)[:, None] * inv_freq[None, :]
    cos, sin = jnp.cos(ang), jnp.sin(ang)
    xr = x[..., :ROPE_DIM].astype(jnp.float32)
    x1, x2 = xr[..., :half], xr[..., half:]
    rot = jnp.concatenate([x1 * cos - x2 * sin, x1 * sin + x2 * cos], axis=-1).astype(x.dtype)
    return jnp.concatenate([rot, x[..., ROPE_DIM:]], axis=-1)


def to_query_blocks(q):
    b, h, s, d = q.shape
    return q.reshape(b, h, s // Q_BLOCK, Q_BLOCK, d).transpose(2, 0, 1, 3, 4)


def from_query_blocks(o):
    n, b, h, qb, d = o.shape
    return o.transpose(1, 2, 0, 3, 4).reshape(b, h, n * qb, d)


def dilated_window_attention(q, k, v):
    scale = HEAD_DIM ** -0.5
    n_blk = q.shape[2] // Q_BLOCK

    def block(args):
        blk, qblk = args
        t = blk * Q_BLOCK + jnp.arange(Q_BLOCK)
        outs, lses = [], []
        for window, dil in DILATED_BRANCHES:
            n_off = window // dil + 1
            idx = t[:, None] - dil * jnp.arange(n_off)[None, :]
            valid = idx >= 0
            idx = jnp.maximum(idx, 0)
            kg = jnp.take(k, idx, axis=2)
            vg = jnp.take(v, idx, axis=2)
            s = jnp.einsum('bhqd,bhqnd->bhqn', qblk, kg).astype(jnp.float32) * scale
            s = jnp.where(valid, s, -jnp.inf)
            lse = jax.nn.logsumexp(s, axis=-1)
            p = jnp.exp(s - lse[..., None])
            outs.append(jnp.einsum('bhqn,bhqnd->bhqd', p.astype(vg.dtype), vg).astype(jnp.float32))
            lses.append(lse)
        w = jax.nn.softmax(jnp.stack(lses, axis=-1), axis=-1)
        o = jnp.einsum('bhqr,rbhqd->bhqd', w, jnp.stack(outs, axis=0))
        return o.astype(q.dtype)

    out = lax.map(block, (jnp.arange(n_blk), to_query_blocks(q)))
    return from_query_blocks(out)


def stick_breaking_attention(q, k, v):
    scale = HEAD_DIM ** -0.5
    s_len = q.shape[2]
    n_blk = s_len // Q_BLOCK
    s_idx = jnp.arange(s_len)

    def block(args):
        blk, qblk = args
        t = blk * Q_BLOCK + jnp.arange(Q_BLOCK)
        causal = s_idx[None, :] < t[:, None]
        z = jnp.einsum('bhqd,bhsd->bhqs', qblk, k).astype(jnp.float32) * scale
        log_beta = jax.nn.log_sigmoid(z)
        log_1m = jnp.where(causal, jax.nn.log_sigmoid(-z), 0.0)
        later = lax.cumsum(log_1m, axis=3, reverse=True) - log_1m
        a = jnp.where(causal, jnp.exp(log_beta + later), 0.0)
        return jnp.einsum('bhqs,bhsd->bhqd', a.astype(v.dtype), v)

    out = lax.map(block, (jnp.arange(n_blk), to_query_blocks(q)))
    return from_query_blocks(out)


def hybrid_mixer(x, w_in, w_out):
    s_len = x.shape[1]
    pos = jnp.arange(s_len)
    proj = x @ w_in
    cuts = [DIL_WIDTH, 2 * DIL_WIDTH, 3 * DIL_WIDTH,
            3 * DIL_WIDTH + SB_WIDTH, 3 * DIL_WIDTH + 2 * SB_WIDTH]
    qa, ka, va, qb, kb, vb = jnp.split(proj, cuts, axis=-1)
    qa = partial_rotary(split_heads(qa, DIL_HEADS), pos)
    ka = partial_rotary(split_heads(ka, DIL_HEADS), pos)
    oa = dilated_window_attention(qa, ka, split_heads(va, DIL_HEADS))
    ob = stick_breaking_attention(split_heads(qb, SB_HEADS), split_heads(kb, SB_HEADS),
                                  split_heads(vb, SB_HEADS))
    o = jnp.concatenate([merge_heads(oa), merge_heads(ob)], axis=-1)
    return o @ w_out


def peer_ffn(x, w_query, sub_keys, expert_u, expert_v):
    b, s_len, d = x.shape
    q = (x @ w_query).reshape(b, s_len, PEER_HEADS, 2, PEER_KEY_DIM // 2)
    scores = jnp.einsum('bshpc,hpnc->bshpn', q, sub_keys).astype(jnp.float32)
    top_s, top_i = lax.top_k(scores, PEER_TOPK)
    cand_s = top_s[..., 0, :, None] + top_s[..., 1, None, :]
    cand_i = top_i[..., 0, :, None] * N_SUBKEYS + top_i[..., 1, None, :]
    cand_s = cand_s.reshape(b, s_len, PEER_HEADS, PEER_TOPK * PEER_TOPK)
    cand_i = cand_i.reshape(b, s_len, PEER_HEADS, PEER_TOPK * PEER_TOPK)
    best_s, best_pos = lax.top_k(cand_s, PEER_TOPK)
    expert_idx = jnp.take_along_axis(cand_i, best_pos, axis=-1)
    gates = jax.nn.softmax(best_s, axis=-1)

    n_tok = b * s_len
    n_blk = n_tok // TOKEN_BLOCK
    n_sel = PEER_HEADS * PEER_TOPK
    xt = x.reshape(n_blk, TOKEN_BLOCK, d)
    it = expert_idx.reshape(n_blk, TOKEN_BLOCK, n_sel)
    gt = gates.reshape(n_blk, TOKEN_BLOCK, n_sel)

    def block(args):
        xb, ib, gb = args
        u = jnp.take(expert_u, ib, axis=0)
        h = jnp.einsum('td,ted->te', xb, u)
        act = gb.astype(xb.dtype) * jax.nn.gelu(h, approximate=False)
        vv = jnp.take(expert_v, ib, axis=0)
        return jnp.einsum('te,ted->td', act, vv)

    y = lax.map(block, (xt, it, gt))
    return y.reshape(b, s_len, d)


def setup_inputs(seed: int = 0) -> dict:
    key = jax.random.key(seed)
    ks = jax.random.split(key, 12)
    f32 = jnp.float32
    x = jax.random.normal(ks[0], (BATCH, SEQ, D_MODEL), f32)
    col_scale = jnp.concatenate([
        jnp.ones((2 * DIL_WIDTH,), f32), jnp.full((DIL_WIDTH,), DEEPNORM_BETA, f32),
        jnp.ones((2 * SB_WIDTH,), f32), jnp.full((SB_WIDTH,), DEEPNORM_BETA, f32)])
    w_in = jax.random.normal(ks[1], (DEPTH, D_MODEL, QKV_WIDTH), f32) * (D_MODEL ** -0.5) * col_scale
    w_out = jax.random.normal(ks[2], (DEPTH, D_MODEL, D_MODEL), f32) * (D_MODEL ** -0.5) * DEEPNORM_BETA
    ln1_g = 1.0 + 0.02 * jax.random.normal(ks[3], (DEPTH, D_MODEL), f32)
    ln1_b = 0.02 * jax.random.normal(ks[4], (DEPTH, D_MODEL), f32)
    peer_wq = jax.random.normal(ks[5], (DEPTH, D_MODEL, PEER_HEADS * PEER_KEY_DIM), f32) * (D_MODEL ** -0.5)
    peer_sub_keys = jax.random.normal(ks[6], (DEPTH, PEER_HEADS, 2, N_SUBKEYS, PEER_KEY_DIM // 2), f32) \
        * ((PEER_KEY_DIM // 2) ** -0.5)
    peer_u = jax.random.normal(ks[7], (DEPTH, N_EXPERTS, D_MODEL), f32) * (D_MODEL ** -0.5)
    peer_v = jax.random.normal(ks[8], (DEPTH, N_EXPERTS, D_MODEL), f32) * DEEPNORM_BETA * (PEER_HEADS ** -0.5)
    ln2_g = 1.0 + 0.02 * jax.random.normal(ks[9], (DEPTH, D_MODEL), f32)
    ln2_b = 0.02 * jax.random.normal(ks[10], (DEPTH, D_MODEL), f32)
    return {"x": x, "w_in": w_in, "w_out": w_out, "ln1_g": ln1_g, "ln1_b": ln1_b,
            "peer_wq": peer_wq, "peer_sub_keys": peer_sub_keys, "peer_u": peer_u,
            "peer_v": peer_v, "ln2_g": ln2_g, "ln2_b": ln2_b}


def reference(x, w_in, w_out, ln1_g, ln1_b, peer_wq, peer_sub_keys, peer_u, peer_v, ln2_g, ln2_b):
    for layer in range(DEPTH):
        mix = hybrid_mixer(x, w_in[layer], w_out[layer])
        x = layer_norm(DEEPNORM_ALPHA * x + mix, ln1_g[layer], ln1_b[layer])
        ffn = peer_ffn(x, peer_wq[layer], peer_sub_keys[layer], peer_u[layer], peer_v[layer])
        x = layer_norm(DEEPNORM_ALPHA * x + ffn, ln2_g[layer], ln2_b[layer])
    return x
```

```python
import functools

import jax
import jax.numpy as jnp
from jax import lax
from jax.experimental import pallas as pl
from jax.experimental.pallas import tpu as pltpu

F32 = jnp.float32
BF16 = jnp.bfloat16

D_MODEL = 1024
HEAD_DIM = 64
GROUP_WIDTH = 512
QKV_WIDTH = 6 * GROUP_WIDTH
DILATIONS = (1, 4, 16)
WINDOW_STEPS = 128
ROPE_DIM = HEAD_DIM // 4
ROPE_THETA = 500000.0
PEER_HEADS = 8
N_SUBKEYS = 128
N_EXPERTS = N_SUBKEYS * N_SUBKEYS
PEER_TOPK = 16
LN_EPS = 1e-5
DEEPNORM_ALPHA = 2.0 ** 0.25
LANES = 128
NEG_BIG = -1e30

VMEM_LIMIT = 56 * 1024 * 1024


def _cparams(*sem):
    return pltpu.CompilerParams(dimension_semantics=sem, vmem_limit_bytes=VMEM_LIMIT)


def _rope_tables(seq_len):
    half = ROPE_DIM // 2
    inv_freq = ROPE_THETA ** (-jnp.arange(half, dtype=F32) * 2.0 / ROPE_DIM)
    ang = jnp.arange(seq_len, dtype=F32)[:, None] * inv_freq[None, :]
    cos, sin = jnp.cos(ang), jnp.sin(ang)
    rest = HEAD_DIM - ROPE_DIM
    ones = jnp.ones((seq_len, rest), F32)
    zeros = jnp.zeros((seq_len, rest), F32)
    zh = jnp.zeros((seq_len, half), F32)
    cos_t = jnp.concatenate([cos, cos, ones], axis=1)
    up_t = jnp.concatenate([-sin, zh, zeros], axis=1)
    dn_t = jnp.concatenate([zh, sin, zeros], axis=1)
    rep = LANES // HEAD_DIM
    return tuple(jnp.tile(t, (1, rep)) for t in (cos_t, up_t, dn_t))


def _qkv_kernel(x_ref, w_ref, cos_ref, up_ref, dn_ref, o_ref):
    xb = x_ref[...].astype(BF16)
    half = ROPE_DIM // 2
    for j in range(6):
        acc = jnp.dot(xb, w_ref[:, j * GROUP_WIDTH:(j + 1) * GROUP_WIDTH],
                      preferred_element_type=F32)
        for c in range(GROUP_WIDTH // LANES):
            a = acc[:, c * LANES:(c + 1) * LANES]
            if j in (0, 1):
                up = pltpu.roll(a, LANES - half, axis=1)
                dn = pltpu.roll(a, half, axis=1)
                a = a * cos_ref[...] + up * up_ref[...] + dn * dn_ref[...]
            if j in (0, 3):
                a = a * (HEAD_DIM ** -0.5)
            o_ref[:, j * GROUP_WIDTH + c * LANES:j * GROUP_WIDTH + (c + 1) * LANES] = a.astype(BF16)


def _qkv_proj(x2, w_in_b, seq_len, tm=512):
    n_tok = x2.shape[0]
    cos_t, up_t, dn_t = _rope_tables(seq_len)
    pos_blocks = seq_len // tm
    tab_spec = pl.BlockSpec((tm, LANES), lambda i: (i % pos_blocks, 0))
    return pl.pallas_call(
        _qkv_kernel,
        grid=(n_tok // tm,),
        in_specs=[pl.BlockSpec((tm, D_MODEL), lambda i: (i, 0)),
                  pl.BlockSpec((D_MODEL, QKV_WIDTH), lambda i: (0, 0)),
                  tab_spec, tab_spec, tab_spec],
        out_specs=pl.BlockSpec((tm, QKV_WIDTH), lambda i: (i, 0)),
        out_shape=jax.ShapeDtypeStruct((n_tok, QKV_WIDTH), BF16),
        compiler_params=_cparams("parallel"),
        name="qkv_proj",
    )(x2, w_in_b, cos_t, up_t, dn_t)


def _dil_kernel(q_ref, k_ref, v_ref, o_ref, lse_ref, *, length):
    tq = LANES
    tk = 2 * LANES
    lane = lax.broadcasted_iota(jnp.int32, (1, LANES), 1)
    head0 = lane < HEAD_DIM
    row = lax.broadcasted_iota(jnp.int32, (tq, tk), 0)
    col = lax.broadcasted_iota(jnp.int32, (tq, tk), 1)
    rel = row - col

    def body(qb, carry):
        q0 = pl.multiple_of(qb * tq, tq)
        k0 = pl.multiple_of(jnp.maximum(q0 - tq, 0), tq)
        q = q_ref[pl.ds(q0, tq), :]
        k = k_ref[pl.ds(k0, tk), :]
        v = v_ref[pl.ds(k0, tk), :]
        diff = rel + (q0 - k0)
        valid = jnp.where(diff >= 0, diff, WINDOW_STEPS + 1) <= WINDOW_STEPS
        outs, lses = [], []
        for h in range(2):
            hm = head0 if h == 0 else jnp.logical_not(head0)
            qh = jnp.where(hm, q, jnp.zeros_like(q))
            s = lax.dot_general(qh, k, (((1,), (1,)), ((), ())),
                                preferred_element_type=F32)
            s = jnp.where(valid, s, NEG_BIG)
            m = jnp.max(s, axis=1, keepdims=True)
            p = jnp.exp(s - m)
            l = jnp.sum(p, axis=1, keepdims=True)
            oh = jnp.dot(p.astype(BF16), v, preferred_element_type=F32)
            outs.append(oh / l)
            lses.append(m + jnp.log(l))
        o_ref[pl.ds(q0, tq), :] = jnp.where(head0, outs[0], outs[1])
        lse_ref[pl.ds(q0, tq), :] = jnp.where(head0, lses[0], lses[1])
        return carry

    lax.fori_loop(0, length // tq, body, 0)


def _dilated_branch(qkv3, dil):
    batch, seq_len, _ = qkv3.shape
    length = seq_len // dil
    view = qkv3.reshape(batch, length, dil * QKV_WIDTH)
    cols = QKV_WIDTH // LANES
    gcols = GROUP_WIDTH // LANES

    def in_spec(group):
        return pl.BlockSpec((None, length, LANES),
                            lambda b, r, hp: (b, 0, r * cols + group * gcols + hp))

    out_spec = pl.BlockSpec((None, length, LANES), lambda b, r, hp: (b, 0, r * gcols + hp))
    out_sds = jax.ShapeDtypeStruct((batch, length, dil * GROUP_WIDTH), F32)
    o, lse = pl.pallas_call(
        functools.partial(_dil_kernel, length=length),
        grid=(batch, dil, gcols),
        in_specs=[in_spec(0), in_spec(1), in_spec(2)],
        out_specs=[out_spec, out_spec],
        out_shape=[out_sds, out_sds],
        compiler_params=_cparams("parallel", "parallel", "parallel"),
        name=f"dilated_attn_d{dil}",
    )(view, view, view)
    n_tok = batch * seq_len
    return o.reshape(n_tok, GROUP_WIDTH), lse.reshape(n_tok, GROUP_WIDTH)


def _sb_kernel(q_ref, k_ref, v_ref, o_ref, acc_ref, run_ref, *, seq_len, blk):
    lane = lax.broadcasted_iota(jnp.int32, (1, LANES), 1)
    head0 = lane < HEAD_DIM
    row = lax.broadcasted_iota(jnp.int32, (blk, blk), 0)
    col = lax.broadcasted_iota(jnp.int32, (blk, blk), 1)
    below = row > col
    tri = jnp.where(below, 1.0, 0.0).astype(BF16)
    tri2 = jnp.concatenate([tri, tri], axis=0)
    reps = blk // LANES

    def q_body(qi, carry):
        q0 = pl.multiple_of(qi * blk, blk)
        q = q_ref[pl.ds(q0, blk), :]
        outs = []
        for h in range(2):
            hm = head0 if h == 0 else jnp.logical_not(head0)
            qh = jnp.where(hm, q, jnp.zeros_like(q))
            acc_ref[...] = jnp.zeros_like(acc_ref)
            run_ref[...] = jnp.zeros_like(run_ref)

            def block(kj, diag):
                k0 = pl.multiple_of(kj * blk, blk)
                k = k_ref[pl.ds(k0, blk), :]
                v = v_ref[pl.ds(k0, blk), :]
                z = lax.dot_general(qh, k, (((1,), (1,)), ((), ())),
                                    preferred_element_type=F32)
                sp = jnp.maximum(z, 0.0) + jnp.log1p(jnp.exp(-jnp.abs(z)))
                log_1m = -sp
                log_b = z - sp
                if diag:
                    log_1m = jnp.where(below, log_1m, 0.0)
                hi = log_1m.astype(BF16)
                lo = (log_1m - hi.astype(F32)).astype(BF16)
                later = jnp.dot(jnp.concatenate([hi, lo], axis=1), tri2,
                                preferred_element_type=F32)
                run = run_ref[...]
                run_w = jnp.concatenate([run] * reps, axis=1)
                a = jnp.exp(log_b + later + run_w)
                if diag:
                    a = jnp.where(below, a, 0.0)
                acc_ref[...] += jnp.dot(a.astype(BF16), v, preferred_element_type=F32)
                run_ref[...] = run + jnp.sum(log_1m, axis=1, keepdims=True)

            block(qi, True)

            def k_body(i, c):
                block(qi - 1 - i, False)
                return c

            lax.fori_loop(0, qi, k_body, 0)
            outs.append(acc_ref[...])
        o_ref[pl.ds(q0, blk), :] = jnp.where(head0, outs[0], outs[1]).astype(o_ref.dtype)
        return carry

    lax.fori_loop(0, seq_len // blk, q_body, 0)


def _stick_breaking(qkv3, blk=256):
    batch, seq_len, _ = qkv3.shape
    gcols = GROUP_WIDTH // LANES

    def in_spec(group):
        return pl.BlockSpec((None, seq_len, LANES),
                            lambda b, hp: (b, 0, (3 + group) * gcols + hp))

    out = pl.pallas_call(
        functools.partial(_sb_kernel, seq_len=seq_len, blk=blk),
        grid=(batch, gcols),
        in_specs=[in_spec(0), in_spec(1), in_spec(2)],
        out_specs=pl.BlockSpec((None, seq_len, LANES), lambda b, hp: (b, 0, hp)),
        out_shape=jax.ShapeDtypeStruct((batch, seq_len, GROUP_WIDTH), BF16),
        scratch_shapes=[pltpu.VMEM((blk, LANES), F32), pltpu.VMEM((blk, LANES), F32)],
        compiler_params=_cparams("parallel", "parallel"),
        name="stick_breaking_attn",
    )(qkv3, qkv3, qkv3)
    return out.reshape(batch * seq_len, GROUP_WIDTH)


def _layer_norm(y, g, b):
    mu = jnp.mean(y, axis=-1, keepdims=True)
    yc = y - mu
    var = jnp.mean(yc * yc, axis=-1, keepdims=True)
    return yc * lax.rsqrt(var + LN_EPS) * g + b


def _outproj_kernel(o1, o2, o3, l1, l2, l3, sb_ref, x_ref, w_ref, g_ref, b_ref,
                    y_ref, yb_ref):
    la, lb, lc = l1[...], l2[...], l3[...]
    mx = jnp.maximum(jnp.maximum(la, lb), lc)
    wa, wb, wc = jnp.exp(la - mx), jnp.exp(lb - mx), jnp.exp(lc - mx)
    oa = (wa * o1[...] + wb * o2[...] + wc * o3[...]) / (wa + wb + wc)
    mix = jnp.dot(oa.astype(BF16), w_ref[:GROUP_WIDTH, :], preferred_element_type=F32)
    mix = mix + jnp.dot(sb_ref[...], w_ref[GROUP_WIDTH:, :], preferred_element_type=F32)
    y = _layer_norm(DEEPNORM_ALPHA * x_ref[...] + mix, g_ref[...], b_ref[...])
    y_ref[...] = y
    yb_ref[...] = y.astype(BF16)


def _outproj_ln(dil_outs, sb_out, x2, w_out_b, g, b, tm=512):
    n_tok = x2.shape[0]
    half_spec = pl.BlockSpec((tm, GROUP_WIDTH), lambda i: (i, 0))
    full_spec = pl.BlockSpec((tm, D_MODEL), lambda i: (i, 0))
    vec_spec = pl.BlockSpec((1, D_MODEL), lambda i: (0, 0))
    os_ = [o for o, _ in dil_outs]
    ls_ = [l for _, l in dil_outs]
    return pl.pallas_call(
        _outproj_kernel,
        grid=(n_tok // tm,),
        in_specs=[half_spec] * 7 + [full_spec,
                                    pl.BlockSpec((D_MODEL, D_MODEL), lambda i: (0, 0)),
                                    vec_spec, vec_spec],
        out_specs=[full_spec, full_spec],
        out_shape=[jax.ShapeDtypeStruct((n_tok, D_MODEL), F32),
                   jax.ShapeDtypeStruct((n_tok, D_MODEL), BF16)],
        compiler_params=_cparams("parallel"),
        name="outproj_ln1",
    )(*os_, *ls_, sb_out, x2, w_out_b, g.reshape(1, D_MODEL), b.reshape(1, D_MODEL))


def _staircase():
    return [(a, b) for a in range(PEER_TOPK) for b in range(PEER_TOPK)
            if (a + 1) * (b + 1) <= PEER_TOPK]


def _peer_select_kernel(x_ref, wq_ref, keys_ref, cnt_out, e0_out, rank_out, e1_out,
                        q_scr, sc_scr, rank_scr, top_scr, cnt_scr, rz_scr, *, tm):
    n_hp = 2 * PEER_HEADS
    n_chunk = tm // LANES
    q_scr[...] = jnp.dot(x_ref[...], wq_ref[...], preferred_element_type=F32).astype(BF16)
    for hp in range(n_hp):
        sc_scr[hp] = lax.dot_general(
            keys_ref[hp], q_scr[:, hp * N_SUBKEYS:(hp + 1) * N_SUBKEYS],
            (((1,), (1,)), ((), ())), preferred_element_type=F32)

    key_id = lax.broadcasted_iota(jnp.int32, (N_SUBKEYS, LANES), 0)

    def stage1(idx, carry):
        hp = idx // n_chunk
        c0 = pl.multiple_of((idx % n_chunk) * LANES, LANES)
        s = sc_scr[hp, :, pl.ds(c0, LANES)]
        rank = jnp.full((N_SUBKEYS, LANES), float(PEER_TOPK), F32)
        for it in range(PEER_TOPK):
            m = jnp.max(s, axis=0, keepdims=True)
            first = jnp.min(jnp.where(s == m, key_id, N_SUBKEYS), axis=0, keepdims=True)
            sel = key_id == first
            rank = jnp.where(sel, float(it), rank)
            s = jnp.where(sel, -jnp.inf, s)
            top_scr[hp, pl.ds(it, 1), pl.ds(c0, LANES)] = m
        rank_scr[hp, :, pl.ds(c0, LANES)] = rank
        return carry

    lax.fori_loop(0, n_hp * n_chunk, stage1, 0)

    pairs = _staircase()

    def stage2(c, carry):
        c0 = pl.multiple_of(c * LANES, LANES)
        def heads_on_sublanes(half, a):
            return jnp.concatenate(
                [top_scr[2 * h + half, pl.ds(a, 1), pl.ds(c0, LANES)] for h in range(PEER_HEADS)],
                axis=0)

        t0 = [heads_on_sublanes(0, a) for a in range(PEER_TOPK)]
        t1 = [heads_on_sublanes(1, b) for b in range(PEER_TOPK)]
        cand = [t0[a] + t1[b] for a, b in pairs]
        best = cand[0]
        cnt = [jnp.zeros((PEER_HEADS, LANES), F32) for _ in range(PEER_TOPK)]
        zsum = jnp.zeros((PEER_HEADS, LANES), F32)
        for it in range(PEER_TOPK):
            m = cand[0]
            for cv in cand[1:]:
                m = jnp.maximum(m, cv)
            zsum = zsum + jnp.exp(m - best)
            open_ = jnp.ones((PEER_HEADS, LANES), F32)
            for k, (a, b) in enumerate(pairs):
                hit = jnp.where(cand[k] == m, open_, 0.0)
                open_ = open_ - hit
                cnt[a] = cnt[a] + hit
                cand[k] = jnp.where(hit > 0.0, -jnp.inf, cand[k])
        for a in range(PEER_TOPK):
            cnt_scr[a, :, pl.ds(c0, LANES)] = cnt[a]
        rz_scr[:, pl.ds(c0, LANES)] = 1.0 / zsum
        return carry

    lax.fori_loop(0, n_chunk, stage2, 0)

    def stage3(c, carry):
        c0 = pl.multiple_of(c * LANES, LANES)
        for head in range(PEER_HEADS):
            rank0 = rank_scr[2 * head, :, pl.ds(c0, LANES)]
            cntk = jnp.zeros((N_SUBKEYS, LANES), F32)
            for a in range(PEER_TOPK):
                ca = cnt_scr[a, pl.ds(head, 1), pl.ds(c0, LANES)]
                cntk = jnp.where(rank0 == float(a), ca, cntk)
            s0 = sc_scr[2 * head, :, pl.ds(c0, LANES)]
            s1 = sc_scr[2 * head + 1, :, pl.ds(c0, LANES)]
            m0 = top_scr[2 * head, pl.ds(0, 1), pl.ds(c0, LANES)]
            m1 = top_scr[2 * head + 1, pl.ds(0, 1), pl.ds(c0, LANES)]
            rz = rz_scr[pl.ds(head, 1), pl.ds(c0, LANES)]
            cnt_out[head, :, pl.ds(c0, LANES)] = cntk
            e0_out[head, :, pl.ds(c0, LANES)] = jnp.exp(s0 - m0) * rz
            rank_out[head, :, pl.ds(c0, LANES)] = rank_scr[2 * head + 1, :, pl.ds(c0, LANES)]
            e1_out[head, :, pl.ds(c0, LANES)] = jnp.exp(s1 - m1)
        return carry

    lax.fori_loop(0, n_chunk, stage3, 0)


def _peer_select(x1b, wq_b, keys_b, tm=256):
    n_tok = x1b.shape[0]
    n_hp = 2 * PEER_HEADS
    qdim = n_hp * N_SUBKEYS
    out_spec = pl.BlockSpec((PEER_HEADS, N_SUBKEYS, tm), lambda i: (0, 0, i))
    out_sds = jax.ShapeDtypeStruct((PEER_HEADS, N_SUBKEYS, n_tok), F32)
    return pl.pallas_call(
        functools.partial(_peer_select_kernel, tm=tm),
        grid=(n_tok // tm,),
        in_specs=[pl.BlockSpec((tm, D_MODEL), lambda i: (i, 0)),
                  pl.BlockSpec((D_MODEL, qdim), lambda i: (0, 0)),
                  pl.BlockSpec((n_hp, N_SUBKEYS, N_SUBKEYS), lambda i: (0, 0, 0))],
        out_specs=[out_spec] * 4,
        out_shape=[out_sds] * 4,
        scratch_shapes=[pltpu.VMEM((tm, qdim), BF16),
                        pltpu.VMEM((n_hp, N_SUBKEYS, tm), F32),
                        pltpu.VMEM((n_hp, N_SUBKEYS, tm), F32),
                        pltpu.VMEM((n_hp, PEER_TOPK, tm), F32),
                        pltpu.VMEM((PEER_TOPK, PEER_HEADS, tm), F32),
                        pltpu.VMEM((PEER_HEADS, tm), F32)],
        compiler_params=_cparams("parallel"),
        name="peer_select",
    )(x1b, wq_b, keys_b)


def _peer_dense_kernel(xb_ref, x_ref, u_ref, vt_ref, cnt_ref, e0_ref, rank_ref, e1_ref,
                       g_ref, b_ref, o_ref, acc_ref, act_ref, *, tm, te):
    ei = pl.program_id(1)
    n_chunk = tm // LANES
    rows = te // N_SUBKEYS

    @pl.when(ei == 0)
    def _():
        acc_ref[...] = jnp.zeros_like(acc_ref)

    ht = lax.dot_general(u_ref[...], xb_ref[...], (((1,), (1,)), ((), ())),
                         preferred_element_type=F32)
    act_ref[...] = ht

    def gate_body(c, carry):
        c0 = pl.multiple_of(c * LANES, LANES)
        for r in range(rows):
            gate = jnp.zeros((N_SUBKEYS, LANES), F32)
            for h in range(PEER_HEADS):
                cnt = cnt_ref[h, pl.ds(r, 1), pl.ds(c0, LANES)]
                e0 = e0_ref[h, pl.ds(r, 1), pl.ds(c0, LANES)]
                rank1 = rank_ref[h, :, pl.ds(c0, LANES)]
                e1 = e1_ref[h, :, pl.ds(c0, LANES)]
                gate = gate + jnp.where(rank1 < cnt, e1 * e0, 0.0)
            hh = act_ref[pl.ds(r * N_SUBKEYS, N_SUBKEYS), pl.ds(c0, LANES)]
            gelu = 0.5 * hh * (1.0 + lax.erf(hh * (2.0 ** -0.5)))
            act_ref[pl.ds(r * N_SUBKEYS, N_SUBKEYS), pl.ds(c0, LANES)] = gate * gelu
        return carry

    lax.fori_loop(0, n_chunk, gate_body, 0)

    acc_ref[...] += jnp.dot(vt_ref[...], act_ref[...].astype(BF16),
                            preferred_element_type=F32)

    @pl.when(ei == pl.num_programs(1) - 1)
    def _():
        ffn = acc_ref[...].T
        o_ref[...] = _layer_norm(DEEPNORM_ALPHA * x_ref[...] + ffn, g_ref[...], b_ref[...])


def _peer_dense(x1b, x1, u_b, vt_b, sel, g, b, tm=512, te=1024):
    n_tok = x1.shape[0]
    sel_spec = pl.BlockSpec((PEER_HEADS, N_SUBKEYS, tm), lambda i, e: (0, 0, i))
    row_spec = pl.BlockSpec((PEER_HEADS, te // N_SUBKEYS, tm), lambda i, e: (0, e, i))
    vec_spec = pl.BlockSpec((1, D_MODEL), lambda i, e: (0, 0))
    return pl.pallas_call(
        functools.partial(_peer_dense_kernel, tm=tm, te=te),
        grid=(n_tok // tm, N_EXPERTS // te),
        in_specs=[pl.BlockSpec((tm, D_MODEL), lambda i, e: (i, 0)),
                  pl.BlockSpec((tm, D_MODEL), lambda i, e: (i, 0)),
                  pl.BlockSpec((te, D_MODEL), lambda i, e: (e, 0)),
                  pl.BlockSpec((D_MODEL, te), lambda i, e: (0, e)),
                  row_spec, row_spec, sel_spec, sel_spec, vec_spec, vec_spec],
        out_specs=pl.BlockSpec((tm, D_MODEL), lambda i, e: (i, 0)),
        out_shape=jax.ShapeDtypeStruct((n_tok, D_MODEL), F32),
        scratch_shapes=[pltpu.VMEM((D_MODEL, tm), F32), pltpu.VMEM((te, tm), F32)],
        compiler_params=_cparams("parallel", "arbitrary"),
        name="peer_dense",
    )(x1b, x1, u_b, vt_b, *sel, g.reshape(1, D_MODEL), b.reshape(1, D_MODEL))


def _layer(x, w_in, w_out, ln1_g, ln1_b, peer_wq, sub_keys, peer_u, peer_v, ln2_g, ln2_b):
    batch, seq_len, _ = x.shape
    n_tok = batch * seq_len
    x2 = x.reshape(n_tok, D_MODEL)

    qkv = _qkv_proj(x2, w_in.astype(BF16), seq_len)
    qkv3 = qkv.reshape(batch, seq_len, QKV_WIDTH)
    dil_outs = [_dilated_branch(qkv3, d) for d in DILATIONS]
    sb_out = _stick_breaking(qkv3)
    x1, x1b = _outproj_ln(dil_outs, sb_out, x2, w_out.astype(BF16), ln1_g, ln1_b)

    keys_b = sub_keys.reshape(2 * PEER_HEADS, N_SUBKEYS, N_SUBKEYS).astype(BF16)
    sel = _peer_select(x1b, peer_wq.astype(BF16), keys_b)
    out = _peer_dense(x1b, x1, peer_u.astype(BF16), peer_v.astype(BF16).T, sel, ln2_g, ln2_b)
    return out.reshape(batch, seq_len, D_MODEL)


def kernel(x, w_in, w_out, ln1_g, ln1_b, peer_wq, peer_sub_keys, peer_u, peer_v, ln2_g, ln2_b):
    depth = w_in.shape[0]
    for layer in range(depth):
        x = _layer(x, w_in[layer], w_out[layer], ln1_g[layer], ln1_b[layer],
                   peer_wq[layer], peer_sub_keys[layer], peer_u[layer], peer_v[layer],
                   ln2_g[layer], ln2_b[layer])
    return x
```

```python
import functools

import jax
import jax.numpy as jnp
from jax import lax
from jax.experimental import pallas as pl
from jax.experimental.pallas import tpu as pltpu

F32 = jnp.float32
BF16 = jnp.bfloat16

D_MODEL = 1024
HEAD_DIM = 64
GROUP_WIDTH = 512
QKV_WIDTH = 6 * GROUP_WIDTH
DILATIONS = (1, 4, 16)
WINDOW_STEPS = 128
ROPE_DIM = HEAD_DIM // 4
ROPE_THETA = 500000.0
PEER_HEADS = 8
N_SUBKEYS = 128
N_EXPERTS = N_SUBKEYS * N_SUBKEYS
PEER_TOPK = 16
LN_EPS = 1e-5
DEEPNORM_ALPHA = 2.0 ** 0.25
LANES = 128
NEG_BIG = -1e30
GATE_TILE_ROWS = 32
SB_EXP_UNDERFLOW = -104.0

VMEM_LIMIT = 56 * 1024 * 1024


def _cparams(*sem):
    return pltpu.CompilerParams(dimension_semantics=sem, vmem_limit_bytes=VMEM_LIMIT)


def _rope_tables(seq_len):
    half = ROPE_DIM // 2
    inv_freq = ROPE_THETA ** (-jnp.arange(half, dtype=F32) * 2.0 / ROPE_DIM)
    ang = jnp.arange(seq_len, dtype=F32)[:, None] * inv_freq[None, :]
    cos, sin = jnp.cos(ang), jnp.sin(ang)
    rest = HEAD_DIM - ROPE_DIM
    ones = jnp.ones((seq_len, rest), F32)
    zeros = jnp.zeros((seq_len, rest), F32)
    zh = jnp.zeros((seq_len, half), F32)
    cos_t = jnp.concatenate([cos, cos, ones], axis=1)
    up_t = jnp.concatenate([-sin, zh, zeros], axis=1)
    dn_t = jnp.concatenate([zh, sin, zeros], axis=1)
    rep = LANES // HEAD_DIM
    return tuple(jnp.tile(t, (1, rep)) for t in (cos_t, up_t, dn_t))


def _qkv_kernel(x_ref, w_ref, cos_ref, up_ref, dn_ref, o_ref):
    xb = x_ref[...].astype(BF16)
    half = ROPE_DIM // 2
    for j in range(6):
        acc = jnp.dot(xb, w_ref[:, j * GROUP_WIDTH:(j + 1) * GROUP_WIDTH],
                      preferred_element_type=F32)
        for c in range(GROUP_WIDTH // LANES):
            a = acc[:, c * LANES:(c + 1) * LANES]
            if j in (0, 1):
                up = pltpu.roll(a, LANES - half, axis=1)
                dn = pltpu.roll(a, half, axis=1)
                a = a * cos_ref[...] + up * up_ref[...] + dn * dn_ref[...]
            if j in (0, 3):
                a = a * (HEAD_DIM ** -0.5)
            o_ref[:, j * GROUP_WIDTH + c * LANES:j * GROUP_WIDTH + (c + 1) * LANES] = a.astype(BF16)


def _qkv_proj(x2, w_in_b, seq_len, tm=512):
    n_tok = x2.shape[0]
    cos_t, up_t, dn_t = _rope_tables(seq_len)
    pos_blocks = seq_len // tm
    tab_spec = pl.BlockSpec((tm, LANES), lambda i: (i % pos_blocks, 0))
    return pl.pallas_call(
        _qkv_kernel,
        grid=(n_tok // tm,),
        in_specs=[pl.BlockSpec((tm, D_MODEL), lambda i: (i, 0)),
                  pl.BlockSpec((D_MODEL, QKV_WIDTH), lambda i: (0, 0)),
                  tab_spec, tab_spec, tab_spec],
        out_specs=pl.BlockSpec((tm, QKV_WIDTH), lambda i: (i, 0)),
        out_shape=jax.ShapeDtypeStruct((n_tok, QKV_WIDTH), BF16),
        compiler_params=_cparams("parallel"),
        name="qkv_proj",
    )(x2, w_in_b, cos_t, up_t, dn_t)


def _dil_kernel(q_ref, k_ref, v_ref, o_ref, lse_ref, *, length):
    tq = LANES
    tk = 2 * LANES
    lane = lax.broadcasted_iota(jnp.int32, (1, LANES), 1)
    head0 = lane < HEAD_DIM
    row = lax.broadcasted_iota(jnp.int32, (tq, tk), 0)
    col = lax.broadcasted_iota(jnp.int32, (tq, tk), 1)
    rel = row - col

    def body(qb, carry):
        q0 = pl.multiple_of(qb * tq, tq)
        k0 = pl.multiple_of(jnp.maximum(q0 - tq, 0), tq)
        q = q_ref[pl.ds(q0, tq), :]
        k = k_ref[pl.ds(k0, tk), :]
        v = v_ref[pl.ds(k0, tk), :]
        diff = rel + (q0 - k0)
        valid = jnp.where(diff >= 0, diff, WINDOW_STEPS + 1) <= WINDOW_STEPS
        outs, lses = [], []
        for h in range(2):
            hm = head0 if h == 0 else jnp.logical_not(head0)
            qh = jnp.where(hm, q, jnp.zeros_like(q))
            s = lax.dot_general(qh, k, (((1,), (1,)), ((), ())),
                                preferred_element_type=F32)
            s = jnp.where(valid, s, NEG_BIG)
            m = jnp.max(s, axis=1, keepdims=True)
            p = jnp.exp(s - m)
            l = jnp.sum(p, axis=1, keepdims=True)
            oh = jnp.dot(p.astype(BF16), v, preferred_element_type=F32)
            outs.append(oh / l)
            lses.append(m + jnp.log(l))
        o_ref[pl.ds(q0, tq), :] = jnp.where(head0, outs[0], outs[1])
        lse_ref[pl.ds(q0, tq), :] = jnp.where(head0, lses[0], lses[1])
        return carry

    lax.fori_loop(0, length // tq, body, 0)


def _dilated_branch(qkv3, dil):
    batch, seq_len, _ = qkv3.shape
    length = seq_len // dil
    view = qkv3.reshape(batch, length, dil * QKV_WIDTH)
    cols = QKV_WIDTH // LANES
    gcols = GROUP_WIDTH // LANES

    def in_spec(group):
        return pl.BlockSpec((None, length, LANES),
                            lambda b, r, hp: (b, 0, r * cols + group * gcols + hp))

    out_spec = pl.BlockSpec((None, length, LANES), lambda b, r, hp: (b, 0, r * gcols + hp))
    out_sds = jax.ShapeDtypeStruct((batch, length, dil * GROUP_WIDTH), F32)
    o, lse = pl.pallas_call(
        functools.partial(_dil_kernel, length=length),
        grid=(batch, dil, gcols),
        in_specs=[in_spec(0), in_spec(1), in_spec(2)],
        out_specs=[out_spec, out_spec],
        out_shape=[out_sds, out_sds],
        compiler_params=_cparams("parallel", "parallel", "parallel"),
        name=f"dilated_attn_d{dil}",
    )(view, view, view)
    n_tok = batch * seq_len
    return o.reshape(n_tok, GROUP_WIDTH), lse.reshape(n_tok, GROUP_WIDTH)


def _sb_kernel(q_ref, k_ref, v_ref, o_ref, acc_ref, run_ref, *, seq_len, blk):
    lane = lax.broadcasted_iota(jnp.int32, (1, LANES), 1)
    head0 = lane < HEAD_DIM
    row = lax.broadcasted_iota(jnp.int32, (blk, blk), 0)
    col = lax.broadcasted_iota(jnp.int32, (blk, blk), 1)
    below = row > col
    tri = jnp.where(below, 1.0, 0.0).astype(BF16)
    tri2 = jnp.concatenate([tri, tri], axis=0)
    reps = blk // LANES

    def q_body(qi, carry):
        q0 = pl.multiple_of(qi * blk, blk)
        q = q_ref[pl.ds(q0, blk), :]
        qh = [jnp.where(head0, q, jnp.zeros_like(q)), jnp.where(head0, jnp.zeros_like(q), q)]
        acc_ref[...] = jnp.zeros_like(acc_ref)
        run_ref[...] = jnp.zeros_like(run_ref)

        def block(kj, diag):
            k0 = pl.multiple_of(kj * blk, blk)
            k = k_ref[pl.ds(k0, blk), :]
            v = v_ref[pl.ds(k0, blk), :]
            tops = []
            for h in range(2):
                z = lax.dot_general(qh[h], k, (((1,), (1,)), ((), ())),
                                    preferred_element_type=F32)
                sp = jnp.maximum(z, 0.0) + jnp.log1p(jnp.exp(-jnp.abs(z)))
                log_1m = -sp
                log_b = z - sp
                if diag:
                    log_1m = jnp.where(below, log_1m, 0.0)
                hi = log_1m.astype(BF16)
                lo = (log_1m - hi.astype(F32)).astype(BF16)
                later = jnp.dot(jnp.concatenate([hi, lo], axis=1), tri2,
                                preferred_element_type=F32)
                run = run_ref[h]
                run_w = jnp.concatenate([run] * reps, axis=1)
                a = jnp.exp(log_b + later + run_w)
                if diag:
                    a = jnp.where(below, a, 0.0)
                acc_ref[h] += jnp.dot(a.astype(BF16), v, preferred_element_type=F32)
                run_new = run + jnp.sum(log_1m, axis=1, keepdims=True)
                run_ref[h] = run_new
                tops.append(jnp.max(run_new))
            return jnp.maximum(tops[0], tops[1])

        top = block(qi, True)

        def k_cond(c):
            return jnp.logical_and(c[0] >= 0, c[1] >= SB_EXP_UNDERFLOW)

        def k_body(c):
            return c[0] - 1, block(c[0], False)

        lax.while_loop(k_cond, k_body, (qi - 1, top))
        o_ref[pl.ds(q0, blk), :] = jnp.where(head0, acc_ref[0], acc_ref[1]).astype(o_ref.dtype)
        return carry

    lax.fori_loop(0, seq_len // blk, q_body, 0)


def _stick_breaking(qkv3, blk=256):
    batch, seq_len, _ = qkv3.shape
    gcols = GROUP_WIDTH // LANES

    def in_spec(group):
        return pl.BlockSpec((None, seq_len, LANES),
                            lambda b, hp: (b, 0, (3 + group) * gcols + hp))

    out = pl.pallas_call(
        functools.partial(_sb_kernel, seq_len=seq_len, blk=blk),
        grid=(batch, gcols),
        in_specs=[in_spec(0), in_spec(1), in_spec(2)],
        out_specs=pl.BlockSpec((None, seq_len, LANES), lambda b, hp: (b, 0, hp)),
        out_shape=jax.ShapeDtypeStruct((batch, seq_len, GROUP_WIDTH), BF16),
        scratch_shapes=[pltpu.VMEM((2, blk, LANES), F32), pltpu.VMEM((2, blk, LANES), F32)],
        compiler_params=_cparams("parallel", "parallel"),
        name="stick_breaking_attn",
    )(qkv3, qkv3, qkv3)
    return out.reshape(batch * seq_len, GROUP_WIDTH)


def _layer_norm(y, g, b):
    mu = jnp.mean(y, axis=-1, keepdims=True)
    yc = y - mu
    var = jnp.mean(yc * yc, axis=-1, keepdims=True)
    return yc * lax.rsqrt(var + LN_EPS) * g + b


def _outproj_kernel(o1, o2, o3, l1, l2, l3, sb_ref, x_ref, w_ref, g_ref, b_ref,
                    y_ref, yb_ref):
    la, lb, lc = l1[...], l2[...], l3[...]
    mx = jnp.maximum(jnp.maximum(la, lb), lc)
    wa, wb, wc = jnp.exp(la - mx), jnp.exp(lb - mx), jnp.exp(lc - mx)
    oa = (wa * o1[...] + wb * o2[...] + wc * o3[...]) / (wa + wb + wc)
    mix = jnp.dot(oa.astype(BF16), w_ref[:GROUP_WIDTH, :], preferred_element_type=F32)
    mix = mix + jnp.dot(sb_ref[...], w_ref[GROUP_WIDTH:, :], preferred_element_type=F32)
    y = _layer_norm(DEEPNORM_ALPHA * x_ref[...] + mix, g_ref[...], b_ref[...])
    y_ref[...] = y
    yb_ref[...] = y.astype(BF16)


def _outproj_ln(dil_outs, sb_out, x2, w_out_b, g, b, tm=512):
    n_tok = x2.shape[0]
    half_spec = pl.BlockSpec((tm, GROUP_WIDTH), lambda i: (i, 0))
    full_spec = pl.BlockSpec((tm, D_MODEL), lambda i: (i, 0))
    vec_spec = pl.BlockSpec((1, D_MODEL), lambda i: (0, 0))
    os_ = [o for o, _ in dil_outs]
    ls_ = [l for _, l in dil_outs]
    return pl.pallas_call(
        _outproj_kernel,
        grid=(n_tok // tm,),
        in_specs=[half_spec] * 7 + [full_spec,
                                    pl.BlockSpec((D_MODEL, D_MODEL), lambda i: (0, 0)),
                                    vec_spec, vec_spec],
        out_specs=[full_spec, full_spec],
        out_shape=[jax.ShapeDtypeStruct((n_tok, D_MODEL), F32),
                   jax.ShapeDtypeStruct((n_tok, D_MODEL), BF16)],
        compiler_params=_cparams("parallel"),
        name="outproj_ln1",
    )(*os_, *ls_, sb_out, x2, w_out_b, g.reshape(1, D_MODEL), b.reshape(1, D_MODEL))


def _staircase():
    return [(a, b) for a in range(PEER_TOPK) for b in range(PEER_TOPK)
            if (a + 1) * (b + 1) <= PEER_TOPK]


def _peer_select_kernel(x_ref, wq_ref, keys_ref, cnt_out, e0_out, rank_out, e1_out,
                        q_scr, sc_scr, rank_scr, top_scr, cnt_scr, rz_scr, *, tm):
    n_hp = 2 * PEER_HEADS
    n_chunk = tm // LANES
    q_scr[...] = jnp.dot(x_ref[...], wq_ref[...], preferred_element_type=F32).astype(BF16)
    for hp in range(n_hp):
        sc = lax.dot_general(
            keys_ref[hp], q_scr[:, hp * N_SUBKEYS:(hp + 1) * N_SUBKEYS],
            (((1,), (1,)), ((), ())), preferred_element_type=F32)
        for c in range(n_chunk):
            sc_scr[hp, c] = sc[:, c * LANES:(c + 1) * LANES]

    key_id = lax.broadcasted_iota(jnp.int32, (N_SUBKEYS, LANES), 0)

    def stage1(idx, carry):
        head = idx // n_chunk
        c = idx % n_chunk
        hps = (2 * head, 2 * head + 1)
        s = [sc_scr[hp, c] for hp in hps]
        rank = [jnp.full((N_SUBKEYS, LANES), float(PEER_TOPK), F32) for _ in hps]
        for it in range(PEER_TOPK):
            for p, hp in enumerate(hps):
                m = jnp.max(s[p], axis=0, keepdims=True)
                first = jnp.min(jnp.where(s[p] == m, key_id, N_SUBKEYS), axis=0, keepdims=True)
                sel = key_id == first
                rank[p] = jnp.where(sel, float(it), rank[p])
                s[p] = jnp.where(sel, -jnp.inf, s[p])
                top_scr[hp, c, pl.ds(it, 1), :] = m
        for p, hp in enumerate(hps):
            rank_scr[hp, c] = rank[p]
        return carry

    lax.fori_loop(0, PEER_HEADS * n_chunk, stage1, 0)

    pairs = _staircase()

    def stage2(c, carry):
        def heads_on_sublanes(half, a):
            return jnp.concatenate(
                [top_scr[2 * h + half, c, pl.ds(a, 1), :] for h in range(PEER_HEADS)], axis=0)

        t0 = [heads_on_sublanes(0, a) for a in range(PEER_TOPK)]
        t1 = [heads_on_sublanes(1, b) for b in range(PEER_TOPK)]
        cand = [t0[a] + t1[b] for a, b in pairs]
        best = cand[0]
        cnt = [jnp.zeros((PEER_HEADS, LANES), F32) for _ in range(PEER_TOPK)]
        zsum = jnp.zeros((PEER_HEADS, LANES), F32)
        for it in range(PEER_TOPK):
            m = cand[0]
            for cv in cand[1:]:
                m = jnp.maximum(m, cv)
            zsum = zsum + jnp.exp(m - best)
            open_ = jnp.ones((PEER_HEADS, LANES), F32)
            for k, (a, b) in enumerate(pairs):
                hit = jnp.where(cand[k] == m, open_, 0.0)
                open_ = open_ - hit
                cnt[a] = cnt[a] + hit
                cand[k] = jnp.where(hit > 0.0, -jnp.inf, cand[k])
        for a in range(PEER_TOPK):
            cnt_scr[c, a] = cnt[a]
        rz_scr[c] = 1.0 / zsum
        return carry

    lax.fori_loop(0, n_chunk, stage2, 0)

    def stage3(c, carry):
        for head in range(PEER_HEADS):
            rank0 = rank_scr[2 * head, c]
            cntk = jnp.zeros((N_SUBKEYS, LANES), F32)
            for a in range(PEER_TOPK):
                ca = cnt_scr[c, a, pl.ds(head, 1), :]
                cntk = jnp.where(rank0 == float(a), ca, cntk)
            m0 = top_scr[2 * head, c, pl.ds(0, 1), :]
            m1 = top_scr[2 * head + 1, c, pl.ds(0, 1), :]
            rz = rz_scr[c, pl.ds(head, 1), :]
            cnt_out[head, c] = cntk
            e0_out[head, c] = jnp.exp(sc_scr[2 * head, c] - m0) * rz
            rank_out[head, c] = rank_scr[2 * head + 1, c]
            e1_out[head, c] = jnp.exp(sc_scr[2 * head + 1, c] - m1)
        return carry

    lax.fori_loop(0, n_chunk, stage3, 0)


def _peer_select(x1b, wq_b, keys_b, tm=256):
    n_tok = x1b.shape[0]
    n_hp = 2 * PEER_HEADS
    qdim = n_hp * N_SUBKEYS
    n_chunk = tm // LANES
    out_spec = pl.BlockSpec((PEER_HEADS, n_chunk, N_SUBKEYS, LANES), lambda i: (0, i, 0, 0))
    out_sds = jax.ShapeDtypeStruct((PEER_HEADS, n_tok // LANES, N_SUBKEYS, LANES), F32)
    return pl.pallas_call(
        functools.partial(_peer_select_kernel, tm=tm),
        grid=(n_tok // tm,),
        in_specs=[pl.BlockSpec((tm, D_MODEL), lambda i: (i, 0)),
                  pl.BlockSpec((D_MODEL, qdim), lambda i: (0, 0)),
                  pl.BlockSpec((n_hp, N_SUBKEYS, N_SUBKEYS), lambda i: (0, 0, 0))],
        out_specs=[out_spec] * 4,
        out_shape=[out_sds] * 4,
        scratch_shapes=[pltpu.VMEM((tm, qdim), BF16),
                        pltpu.VMEM((n_hp, n_chunk, N_SUBKEYS, LANES), F32),
                        pltpu.VMEM((n_hp, n_chunk, N_SUBKEYS, LANES), F32),
                        pltpu.VMEM((n_hp, n_chunk, PEER_TOPK, LANES), F32),
                        pltpu.VMEM((n_chunk, PEER_TOPK, PEER_HEADS, LANES), F32),
                        pltpu.VMEM((n_chunk, PEER_HEADS, LANES), F32)],
        compiler_params=_cparams("parallel"),
        name="peer_select",
    )(x1b, wq_b, keys_b)


def _peer_dense_kernel(xb_ref, x_ref, u_ref, vt_ref, cnt_ref, e0_ref, rank_ref, e1_ref,
                       g_ref, b_ref, o_ref, acc_ref, ht_ref, act_ref, *, tm, te):
    ei = pl.program_id(1)
    n_chunk = tm // LANES
    rows = te // N_SUBKEYS
    jrows = GATE_TILE_ROWS

    @pl.when(ei == 0)
    def _():
        acc_ref[...] = jnp.zeros_like(acc_ref)

    ht_ref[...] = lax.dot_general(u_ref[...], xb_ref[...], (((1,), (1,)), ((), ())),
                                  preferred_element_type=F32)

    for c in range(n_chunk):
        lanes = slice(c * LANES, (c + 1) * LANES)

        def gate_body(jb, carry, c=c, lanes=lanes):
            j0 = pl.multiple_of(jb * jrows, jrows)
            gates = [jnp.zeros((jrows, LANES), F32) for _ in range(rows)]
            for h in range(PEER_HEADS):
                rank1 = rank_ref[h, c, pl.ds(j0, jrows), :]
                e1 = e1_ref[h, c, pl.ds(j0, jrows), :]
                for r in range(rows):
                    cnt = cnt_ref[h, c, pl.ds(r, 1), :]
                    e0 = e0_ref[h, c, pl.ds(r, 1), :]
                    gates[r] = gates[r] + jnp.where(rank1 < cnt, e1 * e0, 0.0)
            for r in range(rows):
                row0 = pl.multiple_of(r * N_SUBKEYS + j0, jrows)
                hh = ht_ref[pl.ds(row0, jrows), lanes]
                gelu = 0.5 * hh * (1.0 + lax.erf(hh * (2.0 ** -0.5)))
                act_ref[pl.ds(row0, jrows), lanes] = (gates[r] * gelu).astype(BF16)
            return carry

        lax.fori_loop(0, N_SUBKEYS // jrows, gate_body, 0)

    acc_ref[...] += jnp.dot(vt_ref[...], act_ref[...],
                            preferred_element_type=F32)

    @pl.when(ei == pl.num_programs(1) - 1)
    def _():
        ffn = acc_ref[...].T
        o_ref[...] = _layer_norm(DEEPNORM_ALPHA * x_ref[...] + ffn, g_ref[...], b_ref[...])


def _peer_dense(x1b, x1, u_b, vt_b, sel, g, b, tm=512, te=1024):
    n_tok = x1.shape[0]
    n_chunk = tm // LANES
    sel_spec = pl.BlockSpec((PEER_HEADS, n_chunk, N_SUBKEYS, LANES), lambda i, e: (0, i, 0, 0))
    row_spec = pl.BlockSpec((PEER_HEADS, n_chunk, te // N_SUBKEYS, LANES),
                            lambda i, e: (0, i, e, 0))
    vec_spec = pl.BlockSpec((1, D_MODEL), lambda i, e: (0, 0))
    return pl.pallas_call(
        functools.partial(_peer_dense_kernel, tm=tm, te=te),
        grid=(n_tok // tm, N_EXPERTS // te),
        in_specs=[pl.BlockSpec((tm, D_MODEL), lambda i, e: (i, 0)),
                  pl.BlockSpec((tm, D_MODEL), lambda i, e: (i, 0)),
                  pl.BlockSpec((te, D_MODEL), lambda i, e: (e, 0)),
                  pl.BlockSpec((D_MODEL, te), lambda i, e: (0, e)),
                  row_spec, row_spec, sel_spec, sel_spec, vec_spec, vec_spec],
        out_specs=pl.BlockSpec((tm, D_MODEL), lambda i, e: (i, 0)),
        out_shape=jax.ShapeDtypeStruct((n_tok, D_MODEL), F32),
        scratch_shapes=[pltpu.VMEM((D_MODEL, tm), F32), pltpu.VMEM((te, tm), F32),
                        pltpu.VMEM((te, tm), BF16)],
        compiler_params=_cparams("parallel", "arbitrary"),
        name="peer_dense",
    )(x1b, x1, u_b, vt_b, *sel, g.reshape(1, D_MODEL), b.reshape(1, D_MODEL))


def _layer(x, w_in, w_out, ln1_g, ln1_b, peer_wq, sub_keys, peer_u, peer_v, ln2_g, ln2_b):
    batch, seq_len, _ = x.shape
    n_tok = batch * seq_len
    x2 = x.reshape(n_tok, D_MODEL)

    qkv = _qkv_proj(x2, w_in.astype(BF16), seq_len)
    qkv3 = qkv.reshape(batch, seq_len, QKV_WIDTH)
    dil_outs = [_dilated_branch(qkv3, d) for d in DILATIONS]
    sb_out = _stick_breaking(qkv3)
    x1, x1b = _outproj_ln(dil_outs, sb_out, x2, w_out.astype(BF16), ln1_g, ln1_b)

    keys_b = sub_keys.reshape(2 * PEER_HEADS, N_SUBKEYS, N_SUBKEYS).astype(BF16)
    sel = _peer_select(x1b, peer_wq.astype(BF16), keys_b)
    out = _peer_dense(x1b, x1, peer_u.astype(BF16), peer_v.astype(BF16).T, sel, ln2_g, ln2_b)
    return out.reshape(batch, seq_len, D_MODEL)


def kernel(x, w_in, w_out, ln1_g, ln1_b, peer_wq, peer_sub_keys, peer_u, peer_v, ln2_g, ln2_b):
    depth = w_in.shape[0]
    for layer in range(depth):
        x = _layer(x, w_in[layer], w_out[layer], ln1_g[layer], ln1_b[layer],
                   peer_wq[layer], peer_sub_keys[layer], peer_u[layer], peer_v[layer],
                   ln2_g[layer], ln2_b[layer])
    return x
```

```python
import functools

import jax
import jax.numpy as jnp
from jax import lax
from jax.experimental import pallas as pl
from jax.experimental.pallas import tpu as pltpu

F32 = jnp.float32
BF16 = jnp.bfloat16

D_MODEL = 1024
HEAD_DIM = 64
GROUP_WIDTH = 512
QKV_WIDTH = 6 * GROUP_WIDTH
DILATIONS = (1, 4, 16)
WINDOW_STEPS = 128
ROPE_DIM = HEAD_DIM // 4
ROPE_THETA = 500000.0
PEER_HEADS = 8
N_SUBKEYS = 128
N_EXPERTS = N_SUBKEYS * N_SUBKEYS
PEER_TOPK = 16
LN_EPS = 1e-5
DEEPNORM_ALPHA = 2.0 ** 0.25
LANES = 128
MXU_COLS = 256
NEG_BIG = -1e30
GATE_TILE_ROWS = 32
SB_EXP_UNDERFLOW = -104.0

VMEM_LIMIT = 56 * 1024 * 1024


def _cparams(*sem):
    return pltpu.CompilerParams(dimension_semantics=sem, vmem_limit_bytes=VMEM_LIMIT)


def _rope_tables(seq_len):
    half = ROPE_DIM // 2
    inv_freq = ROPE_THETA ** (-jnp.arange(half, dtype=F32) * 2.0 / ROPE_DIM)
    ang = jnp.arange(seq_len, dtype=F32)[:, None] * inv_freq[None, :]
    cos, sin = jnp.cos(ang), jnp.sin(ang)
    rest = HEAD_DIM - ROPE_DIM
    ones = jnp.ones((seq_len, rest), F32)
    zeros = jnp.zeros((seq_len, rest), F32)
    zh = jnp.zeros((seq_len, half), F32)
    cos_t = jnp.concatenate([cos, cos, ones], axis=1)
    up_t = jnp.concatenate([-sin, zh, zeros], axis=1)
    dn_t = jnp.concatenate([zh, sin, zeros], axis=1)
    rep = LANES // HEAD_DIM
    return tuple(jnp.tile(t, (1, rep)) for t in (cos_t, up_t, dn_t))


def _qkv_kernel(x_ref, w_ref, cos_ref, up_ref, dn_ref, o_ref, o4_ref, o16_ref, stage_ref):
    tm = x_ref.shape[0]
    xb = x_ref[...].astype(BF16)
    half = ROPE_DIM // 2
    for j in range(6):
        acc = jnp.dot(xb, w_ref[:, j * GROUP_WIDTH:(j + 1) * GROUP_WIDTH],
                      preferred_element_type=F32)
        for c in range(GROUP_WIDTH // LANES):
            a = acc[:, c * LANES:(c + 1) * LANES]
            if j in (0, 1):
                up = pltpu.roll(a, LANES - half, axis=1)
                dn = pltpu.roll(a, half, axis=1)
                a = a * cos_ref[...] + up * up_ref[...] + dn * dn_ref[...]
            if j in (0, 3):
                a = a * (HEAD_DIM ** -0.5)
            col = j * GROUP_WIDTH + c * LANES
            o_ref[:, col:col + LANES] = a.astype(BF16)
            if j < 3:
                stage_ref[col // LANES] = a
    for dil, od_ref in ((4, o4_ref), (16, o16_ref)):
        for r in range(dil):
            for cc in range(stage_ref.shape[0]):
                od_ref[r, :, cc * LANES:(cc + 1) * LANES] = (
                    stage_ref[cc, pl.ds(r, tm // dil, stride=dil), :].astype(BF16))


def _qkv_proj(x2, w_in_b, batch, seq_len, tm=512):
    n_tok = x2.shape[0]
    cos_t, up_t, dn_t = _rope_tables(seq_len)
    pos_blocks = seq_len // tm
    tab_spec = pl.BlockSpec((tm, LANES), lambda i: (i % pos_blocks, 0))
    a_width = 3 * GROUP_WIDTH

    def regrouped(dil):
        spec = pl.BlockSpec((None, dil, tm // dil, a_width),
                            lambda i: (i // pos_blocks, 0, i % pos_blocks, 0))
        sds = jax.ShapeDtypeStruct((batch, dil, seq_len // dil, a_width), BF16)
        return spec, sds

    spec4, sds4 = regrouped(4)
    spec16, sds16 = regrouped(16)
    return pl.pallas_call(
        _qkv_kernel,
        grid=(n_tok // tm,),
        in_specs=[pl.BlockSpec((tm, D_MODEL), lambda i: (i, 0)),
                  pl.BlockSpec((D_MODEL, QKV_WIDTH), lambda i: (0, 0)),
                  tab_spec, tab_spec, tab_spec],
        out_specs=[pl.BlockSpec((tm, QKV_WIDTH), lambda i: (i, 0)), spec4, spec16],
        out_shape=[jax.ShapeDtypeStruct((n_tok, QKV_WIDTH), BF16), sds4, sds16],
        scratch_shapes=[pltpu.VMEM((a_width // LANES, tm, LANES), F32)],
        compiler_params=_cparams("parallel"),
        name="qkv_proj",
    )(x2, w_in_b, cos_t, up_t, dn_t)


def _dil_kernel(q_ref, k_ref, v_ref, o_ref, lse_ref, *, length):
    tq = LANES
    tk = 2 * LANES
    lane = lax.broadcasted_iota(jnp.int32, (1, LANES), 1)
    head0 = lane < HEAD_DIM
    row = lax.broadcasted_iota(jnp.int32, (tq, tk), 0)
    col = lax.broadcasted_iota(jnp.int32, (tq, tk), 1)
    rel = row - col

    def body(qb, carry):
        q0 = pl.multiple_of(qb * tq, tq)
        k0 = pl.multiple_of(jnp.maximum(q0 - tq, 0), tq)
        q = q_ref[pl.ds(q0, tq), :]
        k = k_ref[pl.ds(k0, tk), :]
        v = v_ref[pl.ds(k0, tk), :]
        diff = rel + (q0 - k0)
        valid = jnp.where(diff >= 0, diff, WINDOW_STEPS + 1) <= WINDOW_STEPS
        outs, lses = [], []
        for h in range(2):
            hm = head0 if h == 0 else jnp.logical_not(head0)
            qh = jnp.where(hm, q, jnp.zeros_like(q))
            s = lax.dot_general(qh, k, (((1,), (1,)), ((), ())),
                                preferred_element_type=F32)
            s = jnp.where(valid, s, NEG_BIG)
            m = jnp.max(s, axis=1, keepdims=True)
            p = jnp.exp(s - m)
            l = jnp.sum(p, axis=1, keepdims=True)
            oh = jnp.dot(p.astype(BF16), v, preferred_element_type=F32)
            outs.append(oh / l)
            lses.append(m + jnp.log(l))
        o_ref[pl.ds(q0, tq), :] = jnp.where(head0, outs[0], outs[1])
        lse_ref[pl.ds(q0, tq), :] = jnp.where(head0, lses[0], lses[1])
        return carry

    lax.fori_loop(0, length // tq, body, 0, unroll=2)


def _dilated_branch(qkv_regrouped, dil):
    batch, _, length, _ = qkv_regrouped.shape
    gcols = GROUP_WIDTH // LANES

    def in_spec(group):
        return pl.BlockSpec((None, None, length, LANES),
                            lambda b, r, hp: (b, r, 0, group * gcols + hp))

    out_spec = pl.BlockSpec((None, None, length, LANES), lambda b, r, hp: (b, r, 0, hp))
    out_sds = jax.ShapeDtypeStruct((batch, dil, length, GROUP_WIDTH), F32)
    return pl.pallas_call(
        functools.partial(_dil_kernel, length=length),
        grid=(batch, dil, gcols),
        in_specs=[in_spec(0), in_spec(1), in_spec(2)],
        out_specs=[out_spec, out_spec],
        out_shape=[out_sds, out_sds],
        compiler_params=_cparams("parallel", "parallel", "parallel"),
        name=f"dilated_attn_d{dil}",
    )(qkv_regrouped, qkv_regrouped, qkv_regrouped)


def _sb_kernel(q_ref, k_ref, v_ref, o_ref, acc_ref, run_ref, *, seq_len, blk):
    lane = lax.broadcasted_iota(jnp.int32, (1, LANES), 1)
    head0 = lane < HEAD_DIM
    row = lax.broadcasted_iota(jnp.int32, (blk, blk), 0)
    col = lax.broadcasted_iota(jnp.int32, (blk, blk), 1)
    below = row > col
    tri = jnp.where(below, 1.0, 0.0).astype(BF16)
    tri2 = jnp.concatenate([tri, tri], axis=0)
    reps = blk // LANES

    def q_body(qi, carry):
        q0 = pl.multiple_of(qi * blk, blk)
        q = q_ref[pl.ds(q0, blk), :]
        qh = [jnp.where(head0, q, jnp.zeros_like(q)), jnp.where(head0, jnp.zeros_like(q), q)]
        acc_ref[...] = jnp.zeros_like(acc_ref)
        run_ref[...] = jnp.zeros_like(run_ref)

        def block(kj, diag):
            k0 = pl.multiple_of(kj * blk, blk)
            k = k_ref[pl.ds(k0, blk), :]
            v = v_ref[pl.ds(k0, blk), :]
            tops = []
            for h in range(2):
                z = lax.dot_general(qh[h], k, (((1,), (1,)), ((), ())),
                                    preferred_element_type=F32)
                sp = jnp.maximum(z, 0.0) + jnp.log1p(jnp.exp(-jnp.abs(z)))
                log_1m = -sp
                log_b = z - sp
                if diag:
                    log_1m = jnp.where(below, log_1m, 0.0)
                hi = log_1m.astype(BF16)
                lo = (log_1m - hi.astype(F32)).astype(BF16)
                later = jnp.dot(jnp.concatenate([hi, lo], axis=1), tri2,
                                preferred_element_type=F32)
                run = run_ref[h]
                run_w = jnp.concatenate([run] * reps, axis=1)
                a = jnp.exp(log_b + later + run_w)
                if diag:
                    a = jnp.where(below, a, 0.0)
                acc_ref[h] += jnp.dot(a.astype(BF16), v, preferred_element_type=F32)
                run_new = run + jnp.sum(log_1m, axis=1, keepdims=True)
                run_ref[h] = run_new
                tops.append(jnp.max(run_new))
            return jnp.maximum(tops[0], tops[1])

        top = block(qi, True)

        def k_cond(c):
            return jnp.logical_and(c[0] >= 0, c[1] >= SB_EXP_UNDERFLOW)

        def k_body(c):
            return c[0] - 1, block(c[0], False)

        lax.while_loop(k_cond, k_body, (qi - 1, top))
        o_ref[pl.ds(q0, blk), :] = jnp.where(head0, acc_ref[0], acc_ref[1]).astype(o_ref.dtype)
        return carry

    lax.fori_loop(0, seq_len // blk, q_body, 0)


def _stick_breaking(qkv3, blk=256):
    batch, seq_len, _ = qkv3.shape
    gcols = GROUP_WIDTH // LANES

    def in_spec(group):
        return pl.BlockSpec((None, seq_len, LANES),
                            lambda b, hp: (b, 0, (3 + group) * gcols + hp))

    out = pl.pallas_call(
        functools.partial(_sb_kernel, seq_len=seq_len, blk=blk),
        grid=(batch, gcols),
        in_specs=[in_spec(0), in_spec(1), in_spec(2)],
        out_specs=pl.BlockSpec((None, seq_len, LANES), lambda b, hp: (b, 0, hp)),
        out_shape=jax.ShapeDtypeStruct((batch, seq_len, GROUP_WIDTH), BF16),
        scratch_shapes=[pltpu.VMEM((2, blk, LANES), F32), pltpu.VMEM((2, blk, LANES), F32)],
        compiler_params=_cparams("parallel", "parallel"),
        name="stick_breaking_attn",
    )(qkv3, qkv3, qkv3)
    return out.reshape(batch * seq_len, GROUP_WIDTH)


def _layer_norm(y, g, b):
    mu = jnp.mean(y, axis=-1, keepdims=True)
    yc = y - mu
    var = jnp.mean(yc * yc, axis=-1, keepdims=True)
    return yc * lax.rsqrt(var + LN_EPS) * g + b


def _outproj_kernel(o1, o2, o3, l1, l2, l3, sb_ref, x_ref, w_ref, g_ref, b_ref,
                    y_ref, yb_ref, o2_scr, o3_scr, l2_scr, l3_scr):
    tm = x_ref.shape[0]
    n_col = GROUP_WIDTH // LANES
    for src, dst in ((o2, o2_scr), (l2, l2_scr), (o3, o3_scr), (l3, l3_scr)):
        dil = src.shape[0]
        for r in range(dil):
            for cc in range(n_col):
                dst[cc, pl.ds(r, tm // dil, stride=dil), :] = src[r, :, cc * LANES:(cc + 1) * LANES]
    parts = []
    for cc in range(n_col):
        cols = slice(cc * LANES, (cc + 1) * LANES)
        la, lb, lc = l1[:, cols], l2_scr[cc], l3_scr[cc]
        mx = jnp.maximum(jnp.maximum(la, lb), lc)
        wa, wb, wc = jnp.exp(la - mx), jnp.exp(lb - mx), jnp.exp(lc - mx)
        parts.append((wa * o1[:, cols] + wb * o2_scr[cc] + wc * o3_scr[cc]) / (wa + wb + wc))
    oa = jnp.concatenate(parts, axis=1)
    mix = jnp.dot(oa.astype(BF16), w_ref[:GROUP_WIDTH, :], preferred_element_type=F32)
    mix = mix + jnp.dot(sb_ref[...], w_ref[GROUP_WIDTH:, :], preferred_element_type=F32)
    y = _layer_norm(DEEPNORM_ALPHA * x_ref[...] + mix, g_ref[...], b_ref[...])
    y_ref[...] = y
    yb_ref[...] = y.astype(BF16)


def _outproj_ln(dil_outs, sb_out, x2, w_out_b, g, b, seq_len, tm=512):
    n_tok = x2.shape[0]
    pos_blocks = seq_len // tm
    half_spec = pl.BlockSpec((tm, GROUP_WIDTH), lambda i: (i, 0))
    full_spec = pl.BlockSpec((tm, D_MODEL), lambda i: (i, 0))
    vec_spec = pl.BlockSpec((1, D_MODEL), lambda i: (0, 0))

    def grouped_spec(dil):
        return pl.BlockSpec((None, dil, tm // dil, GROUP_WIDTH),
                            lambda i: (i // pos_blocks, 0, i % pos_blocks, 0))

    (o1, l1), (o2, l2), (o3, l3) = dil_outs
    s2, s3 = grouped_spec(o2.shape[1]), grouped_spec(o3.shape[1])
    return pl.pallas_call(
        _outproj_kernel,
        grid=(n_tok // tm,),
        in_specs=[half_spec, s2, s3, half_spec, s2, s3, half_spec, full_spec,
                  pl.BlockSpec((D_MODEL, D_MODEL), lambda i: (0, 0)),
                  vec_spec, vec_spec],
        out_specs=[full_spec, full_spec],
        out_shape=[jax.ShapeDtypeStruct((n_tok, D_MODEL), F32),
                   jax.ShapeDtypeStruct((n_tok, D_MODEL), BF16)],
        scratch_shapes=[pltpu.VMEM((GROUP_WIDTH // LANES, tm, LANES), F32)] * 4,
        compiler_params=_cparams("parallel"),
        name="outproj_ln1",
    )(o1, o2, o3, l1, l2, l3, sb_out, x2, w_out_b, g.reshape(1, D_MODEL), b.reshape(1, D_MODEL))


def _staircase():
    return [(a, b) for a in range(PEER_TOPK) for b in range(PEER_TOPK)
            if (a + 1) * (b + 1) <= PEER_TOPK]


def _peer_select_kernel(x_ref, wq_ref, keys_ref, cnt_out, e0_out, rank_out, e1_out,
                        q_scr, sc_scr, rank_scr, top_scr, cnt_scr, rz_scr, *, tm):
    n_hp = 2 * PEER_HEADS
    n_chunk = tm // LANES
    q_scr[...] = jnp.dot(x_ref[...], wq_ref[...], preferred_element_type=F32).astype(BF16)
    for hp in range(n_hp):
        sc = lax.dot_general(
            keys_ref[hp], q_scr[:, hp * N_SUBKEYS:(hp + 1) * N_SUBKEYS],
            (((1,), (1,)), ((), ())), preferred_element_type=F32)
        for c in range(n_chunk):
            sc_scr[hp, c] = sc[:, c * LANES:(c + 1) * LANES]

    key_id = lax.broadcasted_iota(jnp.int32, (N_SUBKEYS, LANES), 0)

    def stage1(idx, carry):
        head = idx // n_chunk
        c = idx % n_chunk
        hps = (2 * head, 2 * head + 1)
        s = [sc_scr[hp, c] for hp in hps]
        rank = [jnp.full((N_SUBKEYS, LANES), float(PEER_TOPK), F32) for _ in hps]
        for it in range(PEER_TOPK):
            for p, hp in enumerate(hps):
                m = jnp.max(s[p], axis=0, keepdims=True)
                first = jnp.min(jnp.where(s[p] == m, key_id, N_SUBKEYS), axis=0, keepdims=True)
                sel = key_id == first
                rank[p] = jnp.where(sel, float(it), rank[p])
                s[p] = jnp.where(sel, -jnp.inf, s[p])
                top_scr[hp, c, pl.ds(it, 1), :] = m
        for p, hp in enumerate(hps):
            rank_scr[hp, c] = rank[p]
        return carry

    lax.fori_loop(0, PEER_HEADS * n_chunk, stage1, 0)

    pairs = _staircase()

    def stage2(c, carry):
        def heads_on_sublanes(half, a):
            return jnp.concatenate(
                [top_scr[2 * h + half, c, pl.ds(a, 1), :] for h in range(PEER_HEADS)], axis=0)

        t0 = [heads_on_sublanes(0, a) for a in range(PEER_TOPK)]
        t1 = [heads_on_sublanes(1, b) for b in range(PEER_TOPK)]
        cand = [t0[a] + t1[b] for a, b in pairs]
        best = cand[0]
        cnt = [jnp.zeros((PEER_HEADS, LANES), F32) for _ in range(PEER_TOPK)]
        zsum = jnp.zeros((PEER_HEADS, LANES), F32)
        for it in range(PEER_TOPK):
            m = cand[0]
            for cv in cand[1:]:
                m = jnp.maximum(m, cv)
            zsum = zsum + jnp.exp(m - best)
            open_ = jnp.ones((PEER_HEADS, LANES), F32)
            for k, (a, b) in enumerate(pairs):
                hit = jnp.where(cand[k] == m, open_, 0.0)
                open_ = open_ - hit
                cnt[a] = cnt[a] + hit
                cand[k] = jnp.where(hit > 0.0, -jnp.inf, cand[k])
        for a in range(PEER_TOPK):
            cnt_scr[c, a] = cnt[a]
        rz_scr[c] = 1.0 / zsum
        return carry

    lax.fori_loop(0, n_chunk, stage2, 0)

    def stage3(c, carry):
        for head in range(PEER_HEADS):
            rank0 = rank_scr[2 * head, c]
            cntk = jnp.zeros((N_SUBKEYS, LANES), F32)
            for a in range(PEER_TOPK):
                ca = cnt_scr[c, a, pl.ds(head, 1), :]
                cntk = jnp.where(rank0 == float(a), ca, cntk)
            m0 = top_scr[2 * head, c, pl.ds(0, 1), :]
            m1 = top_scr[2 * head + 1, c, pl.ds(0, 1), :]
            rz = rz_scr[c, pl.ds(head, 1), :]
            cnt_out[head, c] = cntk
            e0_out[head, c] = jnp.exp(sc_scr[2 * head, c] - m0) * rz
            rank_out[head, c] = rank_scr[2 * head + 1, c]
            e1_out[head, c] = jnp.exp(sc_scr[2 * head + 1, c] - m1)
        return carry

    lax.fori_loop(0, n_chunk, stage3, 0)


def _peer_select(x1b, wq_b, keys_b, tm=256):
    n_tok = x1b.shape[0]
    n_hp = 2 * PEER_HEADS
    qdim = n_hp * N_SUBKEYS
    n_chunk = tm // LANES
    out_spec = pl.BlockSpec((PEER_HEADS, n_chunk, N_SUBKEYS, LANES), lambda i: (0, i, 0, 0))
    out_sds = jax.ShapeDtypeStruct((PEER_HEADS, n_tok // LANES, N_SUBKEYS, LANES), F32)
    return pl.pallas_call(
        functools.partial(_peer_select_kernel, tm=tm),
        grid=(n_tok // tm,),
        in_specs=[pl.BlockSpec((tm, D_MODEL), lambda i: (i, 0)),
                  pl.BlockSpec((D_MODEL, qdim), lambda i: (0, 0)),
                  pl.BlockSpec((n_hp, N_SUBKEYS, N_SUBKEYS), lambda i: (0, 0, 0))],
        out_specs=[out_spec] * 4,
        out_shape=[out_sds] * 4,
        scratch_shapes=[pltpu.VMEM((tm, qdim), BF16),
                        pltpu.VMEM((n_hp, n_chunk, N_SUBKEYS, LANES), F32),
                        pltpu.VMEM((n_hp, n_chunk, N_SUBKEYS, LANES), F32),
                        pltpu.VMEM((n_hp, n_chunk, PEER_TOPK, LANES), F32),
                        pltpu.VMEM((n_chunk, PEER_TOPK, PEER_HEADS, LANES), F32),
                        pltpu.VMEM((n_chunk, PEER_HEADS, LANES), F32)],
        compiler_params=_cparams("parallel"),
        name="peer_select",
    )(x1b, wq_b, keys_b)


def _peer_dense_kernel(xb_ref, x_ref, u_ref, vt_ref, cnt_ref, e0_ref, rank_ref, e1_ref,
                       g_ref, b_ref, o_ref, acc_ref, ht_ref, act_ref, *, tm, te):
    ei = pl.program_id(1)
    last = pl.num_programs(1) - 1
    n_chunk = tm // LANES
    rows = te // N_SUBKEYS
    jrows = GATE_TILE_ROWS
    slot = ei % 2
    prev = 1 - slot

    @pl.when(ei == 0)
    def _():
        acc_ref[...] = jnp.zeros_like(acc_ref)
        act_ref[0] = jnp.zeros((te, tm), BF16)
        ht_ref[1] = jnp.zeros((te, tm), F32)

    for c0 in range(0, tm, MXU_COLS):
        ht_ref[slot, :, c0:c0 + MXU_COLS] = lax.dot_general(
            u_ref[...], xb_ref[c0:c0 + MXU_COLS, :], (((1,), (1,)), ((), ())),
            preferred_element_type=F32)
    for c0 in range(0, tm, MXU_COLS):
        acc_ref[:, c0:c0 + MXU_COLS] += jnp.dot(
            vt_ref[...], act_ref[slot, :, c0:c0 + MXU_COLS], preferred_element_type=F32)

    for c in range(n_chunk):
        lanes = slice(c * LANES, (c + 1) * LANES)
        for j0 in range(0, N_SUBKEYS, jrows):
            gates = [jnp.zeros((jrows, LANES), F32) for _ in range(rows)]
            for h in range(PEER_HEADS):
                rank1 = rank_ref[h, c, j0:j0 + jrows, :]
                e1 = e1_ref[h, c, j0:j0 + jrows, :]
                for r in range(rows):
                    cnt = cnt_ref[h, c, r:r + 1, :]
                    e0 = e0_ref[h, c, r:r + 1, :]
                    gates[r] = gates[r] + jnp.where(rank1 < cnt, e1 * e0, 0.0)
            for r in range(rows):
                row0 = r * N_SUBKEYS + j0
                hh = ht_ref[prev, row0:row0 + jrows, lanes]
                gelu = 0.5 * hh * (1.0 + lax.erf(hh * (2.0 ** -0.5)))
                act_ref[prev, row0:row0 + jrows, lanes] = (gates[r] * gelu).astype(BF16)

    @pl.when(ei == last)
    def _():
        ffn = acc_ref[...].T
        o_ref[...] = _layer_norm(DEEPNORM_ALPHA * x_ref[...] + ffn, g_ref[...], b_ref[...])


def _peer_dense(x1b, x1, u_b, vt_b, sel, g, b, tm=512, te=1024):
    n_tok = x1.shape[0]
    n_chunk = tm // LANES
    n_tiles = N_EXPERTS // te
    def tile(e, lag):
        return jnp.clip(e - lag, 0, n_tiles - 1)

    sel_spec = pl.BlockSpec((PEER_HEADS, n_chunk, N_SUBKEYS, LANES), lambda i, e: (0, i, 0, 0))
    row_spec = pl.BlockSpec((PEER_HEADS, n_chunk, te // N_SUBKEYS, LANES),
                            lambda i, e: (0, i, tile(e, 1), 0))
    vec_spec = pl.BlockSpec((1, D_MODEL), lambda i, e: (0, 0))
    return pl.pallas_call(
        functools.partial(_peer_dense_kernel, tm=tm, te=te),
        grid=(n_tok // tm, n_tiles + 2),
        in_specs=[pl.BlockSpec((tm, D_MODEL), lambda i, e: (i, 0)),
                  pl.BlockSpec((tm, D_MODEL), lambda i, e: (i, 0)),
                  pl.BlockSpec((te, D_MODEL), lambda i, e: (tile(e, 0), 0)),
                  pl.BlockSpec((D_MODEL, te), lambda i, e: (0, tile(e, 2))),
                  row_spec, row_spec, sel_spec, sel_spec, vec_spec, vec_spec],
        out_specs=pl.BlockSpec((tm, D_MODEL), lambda i, e: (i, 0)),
        out_shape=jax.ShapeDtypeStruct((n_tok, D_MODEL), F32),
        scratch_shapes=[pltpu.VMEM((D_MODEL, tm), F32), pltpu.VMEM((2, te, tm), F32),
                        pltpu.VMEM((2, te, tm), BF16)],
        compiler_params=_cparams("parallel", "arbitrary"),
        name="peer_dense",
    )(x1b, x1, u_b, vt_b, *sel, g.reshape(1, D_MODEL), b.reshape(1, D_MODEL))


def _layer(x, w_in, w_out, ln1_g, ln1_b, peer_wq, sub_keys, peer_u, peer_v, ln2_g, ln2_b):
    batch, seq_len, _ = x.shape
    n_tok = batch * seq_len
    x2 = x.reshape(n_tok, D_MODEL)

    qkv, qkv_r4, qkv_r16 = _qkv_proj(x2, w_in.astype(BF16), batch, seq_len)
    qkv3 = qkv.reshape(batch, seq_len, QKV_WIDTH)
    o1, l1 = _dilated_branch(qkv3.reshape(batch, 1, seq_len, QKV_WIDTH), DILATIONS[0])
    dil_outs = [(o1.reshape(n_tok, GROUP_WIDTH), l1.reshape(n_tok, GROUP_WIDTH)),
                _dilated_branch(qkv_r4, DILATIONS[1]),
                _dilated_branch(qkv_r16, DILATIONS[2])]
    sb_out = _stick_breaking(qkv3)
    x1, x1b = _outproj_ln(dil_outs, sb_out, x2, w_out.astype(BF16), ln1_g, ln1_b, seq_len)

    keys_b = sub_keys.reshape(2 * PEER_HEADS, N_SUBKEYS, N_SUBKEYS).astype(BF16)
    sel = _peer_select(x1b, peer_wq.astype(BF16), keys_b)
    out = _peer_dense(x1b, x1, peer_u.astype(BF16), peer_v.astype(BF16).T, sel, ln2_g, ln2_b)
    return out.reshape(batch, seq_len, D_MODEL)


def kernel(x, w_in, w_out, ln1_g, ln1_b, peer_wq, peer_sub_keys, peer_u, peer_v, ln2_g, ln2_b):
    depth = w_in.shape[0]
    for layer in range(depth):
        x = _layer(x, w_in[layer], w_out[layer], ln1_g[layer], ln1_b[layer],
                   peer_wq[layer], peer_sub_keys[layer], peer_u[layer], peer_v[layer],
                   ln2_g[layer], ln2_b[layer])
    return x
```

```python
import functools

import jax
import jax.numpy as jnp
from jax import lax
from jax.experimental import pallas as pl
from jax.experimental.pallas import tpu as pltpu

F32 = jnp.float32
BF16 = jnp.bfloat16

D_MODEL = 1024
HEAD_DIM = 64
GROUP_WIDTH = 512
QKV_WIDTH = 6 * GROUP_WIDTH
DILATIONS = (1, 4, 16)
WINDOW_STEPS = 128
ROPE_DIM = HEAD_DIM // 4
ROPE_THETA = 500000.0
PEER_HEADS = 8
N_SUBKEYS = 128
N_EXPERTS = N_SUBKEYS * N_SUBKEYS
PEER_TOPK = 16
LN_EPS = 1e-5
DEEPNORM_ALPHA = 2.0 ** 0.25
LANES = 128
MXU_COLS = 256
NEG_BIG = -1e30
GATE_TILE_ROWS = 32
SB_EXP_UNDERFLOW = -104.0

VMEM_LIMIT = 56 * 1024 * 1024


def _cparams(*sem):
    return pltpu.CompilerParams(dimension_semantics=sem, vmem_limit_bytes=VMEM_LIMIT)


def _rope_tables(seq_len):
    half = ROPE_DIM // 2
    inv_freq = ROPE_THETA ** (-jnp.arange(half, dtype=F32) * 2.0 / ROPE_DIM)
    ang = jnp.arange(seq_len, dtype=F32)[:, None] * inv_freq[None, :]
    cos, sin = jnp.cos(ang), jnp.sin(ang)
    rest = HEAD_DIM - ROPE_DIM
    ones = jnp.ones((seq_len, rest), F32)
    zeros = jnp.zeros((seq_len, rest), F32)
    zh = jnp.zeros((seq_len, half), F32)
    cos_t = jnp.concatenate([cos, cos, ones], axis=1)
    up_t = jnp.concatenate([-sin, zh, zeros], axis=1)
    dn_t = jnp.concatenate([zh, sin, zeros], axis=1)
    rep = LANES // HEAD_DIM
    return tuple(jnp.tile(t, (1, rep)) for t in (cos_t, up_t, dn_t))


def _qkv_kernel(x_ref, w_ref, cos_ref, up_ref, dn_ref, o_ref, o4_ref, o16_ref, stage_ref):
    tm = x_ref.shape[0]
    xb = x_ref[...].astype(BF16)
    half = ROPE_DIM // 2
    for j in range(6):
        acc = jnp.dot(xb, w_ref[:, j * GROUP_WIDTH:(j + 1) * GROUP_WIDTH],
                      preferred_element_type=F32)
        for c in range(GROUP_WIDTH // LANES):
            a = acc[:, c * LANES:(c + 1) * LANES]
            if j in (0, 1):
                up = pltpu.roll(a, LANES - half, axis=1)
                dn = pltpu.roll(a, half, axis=1)
                a = a * cos_ref[...] + up * up_ref[...] + dn * dn_ref[...]
            if j in (0, 3):
                a = a * (HEAD_DIM ** -0.5)
            col = j * GROUP_WIDTH + c * LANES
            o_ref[:, col:col + LANES] = a.astype(BF16)
            if j < 3:
                stage_ref[col // LANES] = a
    for dil, od_ref in ((4, o4_ref), (16, o16_ref)):
        for r in range(dil):
            for cc in range(stage_ref.shape[0]):
                od_ref[r, :, cc * LANES:(cc + 1) * LANES] = (
                    stage_ref[cc, pl.ds(r, tm // dil, stride=dil), :].astype(BF16))


def _qkv_proj(x2, w_in_b, batch, seq_len, tm=512):
    n_tok = x2.shape[0]
    cos_t, up_t, dn_t = _rope_tables(seq_len)
    pos_blocks = seq_len // tm
    tab_spec = pl.BlockSpec((tm, LANES), lambda i: (i % pos_blocks, 0))
    a_width = 3 * GROUP_WIDTH

    def regrouped(dil):
        spec = pl.BlockSpec((None, dil, tm // dil, a_width),
                            lambda i: (i // pos_blocks, 0, i % pos_blocks, 0))
        sds = jax.ShapeDtypeStruct((batch, dil, seq_len // dil, a_width), BF16)
        return spec, sds

    spec4, sds4 = regrouped(4)
    spec16, sds16 = regrouped(16)
    return pl.pallas_call(
        _qkv_kernel,
        grid=(n_tok // tm,),
        in_specs=[pl.BlockSpec((tm, D_MODEL), lambda i: (i, 0)),
                  pl.BlockSpec((D_MODEL, QKV_WIDTH), lambda i: (0, 0)),
                  tab_spec, tab_spec, tab_spec],
        out_specs=[pl.BlockSpec((tm, QKV_WIDTH), lambda i: (i, 0)), spec4, spec16],
        out_shape=[jax.ShapeDtypeStruct((n_tok, QKV_WIDTH), BF16), sds4, sds16],
        scratch_shapes=[pltpu.VMEM((a_width // LANES, tm, LANES), F32)],
        compiler_params=_cparams("parallel"),
        name="qkv_proj",
    )(x2, w_in_b, cos_t, up_t, dn_t)


def _dil_kernel(q_ref, k_ref, v_ref, o_ref, lse_ref, *, length):
    tq = LANES
    tk = 2 * LANES
    lane = lax.broadcasted_iota(jnp.int32, (1, LANES), 1)
    head0 = lane < HEAD_DIM
    row = lax.broadcasted_iota(jnp.int32, (tq, tk), 0)
    col = lax.broadcasted_iota(jnp.int32, (tq, tk), 1)
    rel = row - col

    n_blocks = length // tq
    n_res = q_ref.shape[0]

    def body(idx, carry):
        res = idx // n_blocks
        q0 = pl.multiple_of((idx % n_blocks) * tq, tq)
        k0 = pl.multiple_of(jnp.maximum(q0 - tq, 0), tq)
        q = q_ref[res, pl.ds(q0, tq), :]
        k = k_ref[res, pl.ds(k0, tk), :]
        v = v_ref[res, pl.ds(k0, tk), :]
        diff = rel + (q0 - k0)
        valid = jnp.where(diff >= 0, diff, WINDOW_STEPS + 1) <= WINDOW_STEPS
        outs, lses = [], []
        for h in range(2):
            hm = head0 if h == 0 else jnp.logical_not(head0)
            qh = jnp.where(hm, q, jnp.zeros_like(q))
            s = lax.dot_general(qh, k, (((1,), (1,)), ((), ())),
                                preferred_element_type=F32)
            s = jnp.where(valid, s, NEG_BIG)
            m = jnp.max(s, axis=1, keepdims=True)
            p = jnp.exp(s - m)
            l = jnp.sum(p, axis=1, keepdims=True)
            oh = jnp.dot(p.astype(BF16), v, preferred_element_type=F32)
            outs.append(oh / l)
            lses.append(m + jnp.log(l))
        o_ref[res, pl.ds(q0, tq), :] = jnp.where(head0, outs[0], outs[1])
        lse_ref[res, pl.ds(q0, tq), :] = jnp.where(head0, lses[0], lses[1])
        return carry

    lax.fori_loop(0, n_res * n_blocks, body, 0, unroll=4)


def _dilated_branch(qkv_regrouped, dil):
    batch, _, length, _ = qkv_regrouped.shape
    gcols = GROUP_WIDTH // LANES
    n_res = min(dil, max(1, (8 * LANES) // length))

    def in_spec(group):
        return pl.BlockSpec((None, n_res, length, LANES),
                            lambda b, r, hp: (b, r, 0, group * gcols + hp))

    out_spec = pl.BlockSpec((None, n_res, length, LANES), lambda b, r, hp: (b, r, 0, hp))
    out_sds = jax.ShapeDtypeStruct((batch, dil, length, GROUP_WIDTH), F32)
    return pl.pallas_call(
        functools.partial(_dil_kernel, length=length),
        grid=(batch, dil // n_res, gcols),
        in_specs=[in_spec(0), in_spec(1), in_spec(2)],
        out_specs=[out_spec, out_spec],
        out_shape=[out_sds, out_sds],
        compiler_params=_cparams("parallel", "parallel", "parallel"),
        name=f"dilated_attn_d{dil}",
    )(qkv_regrouped, qkv_regrouped, qkv_regrouped)


def _sb_kernel(q_ref, k_ref, v_ref, o_ref, acc_ref, run_ref, *, seq_len, blk):
    lane = lax.broadcasted_iota(jnp.int32, (1, LANES), 1)
    head0 = lane < HEAD_DIM
    row = lax.broadcasted_iota(jnp.int32, (blk, blk), 0)
    col = lax.broadcasted_iota(jnp.int32, (blk, blk), 1)
    below = row > col
    tri = jnp.where(below, 1.0, 0.0).astype(BF16)
    tri2 = jnp.concatenate([tri, tri], axis=0)
    reps = blk // LANES

    def q_body(qi, carry):
        q0 = pl.multiple_of(qi * blk, blk)
        q = q_ref[pl.ds(q0, blk), :]
        qh = [jnp.where(head0, q, jnp.zeros_like(q)), jnp.where(head0, jnp.zeros_like(q), q)]
        acc_ref[...] = jnp.zeros_like(acc_ref)
        run_ref[...] = jnp.zeros_like(run_ref)

        def blocks(sweep):
            kv = []
            for kj, diag in sweep:
                k0 = pl.multiple_of(kj * blk, blk)
                kv.append((k_ref[pl.ds(k0, blk), :], v_ref[pl.ds(k0, blk), :], diag))
            tops = []
            for h in range(2):
                parts = []
                for k, v, diag in kv:
                    z = lax.dot_general(qh[h], k, (((1,), (1,)), ((), ())),
                                        preferred_element_type=F32)
                    sp = jnp.maximum(z, 0.0) + jnp.log1p(jnp.exp(-jnp.abs(z)))
                    log_1m = -sp
                    log_b = z - sp
                    if diag:
                        log_1m = jnp.where(below, log_1m, 0.0)
                    hi = log_1m.astype(BF16)
                    lo = (log_1m - hi.astype(F32)).astype(BF16)
                    later = jnp.dot(jnp.concatenate([hi, lo], axis=1), tri2,
                                    preferred_element_type=F32)
                    parts.append((log_b + later, jnp.sum(log_1m, axis=1, keepdims=True), v, diag))
                run = run_ref[h]
                for pre, row_sum, v, diag in parts:
                    run_w = jnp.concatenate([run] * reps, axis=1)
                    a = jnp.exp(pre + run_w)
                    if diag:
                        a = jnp.where(below, a, 0.0)
                    acc_ref[h] += jnp.dot(a.astype(BF16), v, preferred_element_type=F32)
                    run = run + row_sum
                run_ref[h] = run
                tops.append(jnp.max(run))
            return jnp.maximum(tops[0], tops[1])

        top = lax.cond(qi > 0,
                       lambda: blocks([(qi, True), (qi - 1, False)]),
                       lambda: blocks([(qi, True)]))

        def k_cond(c):
            return jnp.logical_and(c[0] >= 0, c[1] >= SB_EXP_UNDERFLOW)

        def k_body(c):
            return c[0] - 1, blocks([(c[0], False)])

        lax.while_loop(k_cond, k_body, (qi - 2, top))
        o_ref[pl.ds(q0, blk), :] = jnp.where(head0, acc_ref[0], acc_ref[1]).astype(o_ref.dtype)
        return carry

    lax.fori_loop(0, seq_len // blk, q_body, 0)


def _stick_breaking(qkv3, blk=256):
    batch, seq_len, _ = qkv3.shape
    gcols = GROUP_WIDTH // LANES

    def in_spec(group):
        return pl.BlockSpec((None, seq_len, LANES),
                            lambda b, hp: (b, 0, (3 + group) * gcols + hp))

    out = pl.pallas_call(
        functools.partial(_sb_kernel, seq_len=seq_len, blk=blk),
        grid=(batch, gcols),
        in_specs=[in_spec(0), in_spec(1), in_spec(2)],
        out_specs=pl.BlockSpec((None, seq_len, LANES), lambda b, hp: (b, 0, hp)),
        out_shape=jax.ShapeDtypeStruct((batch, seq_len, GROUP_WIDTH), BF16),
        scratch_shapes=[pltpu.VMEM((2, blk, LANES), F32), pltpu.VMEM((2, blk, LANES), F32)],
        compiler_params=_cparams("parallel", "parallel"),
        name="stick_breaking_attn",
    )(qkv3, qkv3, qkv3)
    return out.reshape(batch * seq_len, GROUP_WIDTH)


def _layer_norm(y, g, b):
    mu = jnp.mean(y, axis=-1, keepdims=True)
    yc = y - mu
    var = jnp.mean(yc * yc, axis=-1, keepdims=True)
    return yc * lax.rsqrt(var + LN_EPS) * g + b


def _outproj_kernel(o1, o2, o3, l1, l2, l3, sb_ref, x_ref, w_ref, g_ref, b_ref,
                    y_ref, yb_ref, o2_scr, o3_scr, l2_scr, l3_scr):
    tm = x_ref.shape[0]
    n_col = GROUP_WIDTH // LANES
    for src, dst in ((o2, o2_scr), (l2, l2_scr), (o3, o3_scr), (l3, l3_scr)):
        dil = src.shape[0]
        for r in range(dil):
            for cc in range(n_col):
                dst[cc, pl.ds(r, tm // dil, stride=dil), :] = src[r, :, cc * LANES:(cc + 1) * LANES]
    parts = []
    for cc in range(n_col):
        cols = slice(cc * LANES, (cc + 1) * LANES)
        la, lb, lc = l1[:, cols], l2_scr[cc], l3_scr[cc]
        mx = jnp.maximum(jnp.maximum(la, lb), lc)
        wa, wb, wc = jnp.exp(la - mx), jnp.exp(lb - mx), jnp.exp(lc - mx)
        parts.append((wa * o1[:, cols] + wb * o2_scr[cc] + wc * o3_scr[cc]) / (wa + wb + wc))
    oa = jnp.concatenate(parts, axis=1)
    mix = jnp.dot(oa.astype(BF16), w_ref[:GROUP_WIDTH, :], preferred_element_type=F32)
    mix = mix + jnp.dot(sb_ref[...], w_ref[GROUP_WIDTH:, :], preferred_element_type=F32)
    y = _layer_norm(DEEPNORM_ALPHA * x_ref[...] + mix, g_ref[...], b_ref[...])
    y_ref[...] = y
    yb_ref[...] = y.astype(BF16)


def _outproj_ln(dil_outs, sb_out, x2, w_out_b, g, b, seq_len, tm=512):
    n_tok = x2.shape[0]
    pos_blocks = seq_len // tm
    half_spec = pl.BlockSpec((tm, GROUP_WIDTH), lambda i: (i, 0))
    full_spec = pl.BlockSpec((tm, D_MODEL), lambda i: (i, 0))
    vec_spec = pl.BlockSpec((1, D_MODEL), lambda i: (0, 0))

    def grouped_spec(dil):
        return pl.BlockSpec((None, dil, tm // dil, GROUP_WIDTH),
                            lambda i: (i // pos_blocks, 0, i % pos_blocks, 0))

    (o1, l1), (o2, l2), (o3, l3) = dil_outs
    s2, s3 = grouped_spec(o2.shape[1]), grouped_spec(o3.shape[1])
    return pl.pallas_call(
        _outproj_kernel,
        grid=(n_tok // tm,),
        in_specs=[half_spec, s2, s3, half_spec, s2, s3, half_spec, full_spec,
                  pl.BlockSpec((D_MODEL, D_MODEL), lambda i: (0, 0)),
                  vec_spec, vec_spec],
        out_specs=[full_spec, full_spec],
        out_shape=[jax.ShapeDtypeStruct((n_tok, D_MODEL), F32),
                   jax.ShapeDtypeStruct((n_tok, D_MODEL), BF16)],
        scratch_shapes=[pltpu.VMEM((GROUP_WIDTH // LANES, tm, LANES), F32)] * 4,
        compiler_params=_cparams("parallel"),
        name="outproj_ln1",
    )(o1, o2, o3, l1, l2, l3, sb_out, x2, w_out_b, g.reshape(1, D_MODEL), b.reshape(1, D_MODEL))


def _staircase():
    return [(a, b) for a in range(PEER_TOPK) for b in range(PEER_TOPK)
            if (a + 1) * (b + 1) <= PEER_TOPK]


def _peer_select_kernel(x_ref, wq_ref, keys_ref, cnt_out, e0_out, rank_out, e1_out,
                        q_scr, sc_scr, rank_scr, top_scr, cnt_scr, rz_scr, *, tm):
    n_hp = 2 * PEER_HEADS
    n_chunk = tm // LANES
    q_scr[...] = jnp.dot(x_ref[...], wq_ref[...], preferred_element_type=F32).astype(BF16)
    for hp in range(n_hp):
        sc = lax.dot_general(
            keys_ref[hp], q_scr[:, hp * N_SUBKEYS:(hp + 1) * N_SUBKEYS],
            (((1,), (1,)), ((), ())), preferred_element_type=F32)
        for c in range(n_chunk):
            sc_scr[hp, c] = sc[:, c * LANES:(c + 1) * LANES]

    key_id = lax.broadcasted_iota(jnp.int32, (N_SUBKEYS, LANES), 0)

    def stage1(idx, carry):
        head = idx // n_chunk
        c = idx % n_chunk
        hps = (2 * head, 2 * head + 1)
        s = [sc_scr[hp, c] for hp in hps]
        rank = [jnp.full((N_SUBKEYS, LANES), float(PEER_TOPK), F32) for _ in hps]
        for it in range(PEER_TOPK):
            for p, hp in enumerate(hps):
                m = jnp.max(s[p], axis=0, keepdims=True)
                first = jnp.min(jnp.where(s[p] == m, key_id, N_SUBKEYS), axis=0, keepdims=True)
                sel = key_id == first
                rank[p] = jnp.where(sel, float(it), rank[p])
                s[p] = jnp.where(sel, -jnp.inf, s[p])
                top_scr[hp, c, pl.ds(it, 1), :] = m
        for p, hp in enumerate(hps):
            rank_scr[hp, c] = rank[p]
        return carry

    lax.fori_loop(0, PEER_HEADS * n_chunk, stage1, 0)

    pairs = _staircase()

    def stage2(c, carry):
        def heads_on_sublanes(half, a):
            return jnp.concatenate(
                [top_scr[2 * h + half, c, pl.ds(a, 1), :] for h in range(PEER_HEADS)], axis=0)

        t0 = [heads_on_sublanes(0, a) for a in range(PEER_TOPK)]
        t1 = [heads_on_sublanes(1, b) for b in range(PEER_TOPK)]
        cand = [t0[a] + t1[b] for a, b in pairs]
        best = cand[0]
        cnt = [jnp.zeros((PEER_HEADS, LANES), F32) for _ in range(PEER_TOPK)]
        zsum = jnp.zeros((PEER_HEADS, LANES), F32)
        for it in range(PEER_TOPK):
            m = cand[0]
            for cv in cand[1:]:
                m = jnp.maximum(m, cv)
            zsum = zsum + jnp.exp(m - best)
            open_ = jnp.ones((PEER_HEADS, LANES), F32)
            for k, (a, b) in enumerate(pairs):
                hit = jnp.where(cand[k] == m, open_, 0.0)
                open_ = open_ - hit
                cnt[a] = cnt[a] + hit
                cand[k] = jnp.where(hit > 0.0, -jnp.inf, cand[k])
        for a in range(PEER_TOPK):
            cnt_scr[c, a] = cnt[a]
        rz_scr[c] = 1.0 / zsum
        return carry

    lax.fori_loop(0, n_chunk, stage2, 0)

    def stage3(c, carry):
        for head in range(PEER_HEADS):
            rank0 = rank_scr[2 * head, c]
            cntk = jnp.zeros((N_SUBKEYS, LANES), F32)
            for a in range(PEER_TOPK):
                ca = cnt_scr[c, a, pl.ds(head, 1), :]
                cntk = jnp.where(rank0 == float(a), ca, cntk)
            m0 = top_scr[2 * head, c, pl.ds(0, 1), :]
            m1 = top_scr[2 * head + 1, c, pl.ds(0, 1), :]
            rz = rz_scr[c, pl.ds(head, 1), :]
            cnt_out[head, c] = cntk
            e0_out[head, c] = jnp.exp(sc_scr[2 * head, c] - m0) * rz
            rank_out[head, c] = rank_scr[2 * head + 1, c]
            e1_out[head, c] = jnp.exp(sc_scr[2 * head + 1, c] - m1)
        return carry

    lax.fori_loop(0, n_chunk, stage3, 0)


def _peer_select(x1b, wq_b, keys_b, tm=256):
    n_tok = x1b.shape[0]
    n_hp = 2 * PEER_HEADS
    qdim = n_hp * N_SUBKEYS
    n_chunk = tm // LANES
    out_spec = pl.BlockSpec((PEER_HEADS, n_chunk, N_SUBKEYS, LANES), lambda i: (0, i, 0, 0))
    out_sds = jax.ShapeDtypeStruct((PEER_HEADS, n_tok // LANES, N_SUBKEYS, LANES), F32)
    return pl.pallas_call(
        functools.partial(_peer_select_kernel, tm=tm),
        grid=(n_tok // tm,),
        in_specs=[pl.BlockSpec((tm, D_MODEL), lambda i: (i, 0)),
                  pl.BlockSpec((D_MODEL, qdim), lambda i: (0, 0)),
                  pl.BlockSpec((n_hp, N_SUBKEYS, N_SUBKEYS), lambda i: (0, 0, 0))],
        out_specs=[out_spec] * 4,
        out_shape=[out_sds] * 4,
        scratch_shapes=[pltpu.VMEM((tm, qdim), BF16),
                        pltpu.VMEM((n_hp, n_chunk, N_SUBKEYS, LANES), F32),
                        pltpu.VMEM((n_hp, n_chunk, N_SUBKEYS, LANES), F32),
                        pltpu.VMEM((n_hp, n_chunk, PEER_TOPK, LANES), F32),
                        pltpu.VMEM((n_chunk, PEER_TOPK, PEER_HEADS, LANES), F32),
                        pltpu.VMEM((n_chunk, PEER_HEADS, LANES), F32)],
        compiler_params=_cparams("parallel"),
        name="peer_select",
    )(x1b, wq_b, keys_b)


def _peer_dense_kernel(x_ref, u_ref, vt_ref, cnt_ref, e0_ref, rank_ref, e1_ref,
                       g_ref, b_ref, o_ref, acc_ref, xt_ref, ht_ref, act_ref, *, tm, te):
    ei = pl.program_id(1)
    n_chunk = tm // LANES
    rows = te // N_SUBKEYS
    jrows = GATE_TILE_ROWS

    @pl.when(ei == 0)
    def _():
        acc_ref[...] = jnp.zeros_like(acc_ref)
        xt_ref[...] = x_ref[...].T.astype(BF16)

    ht_ref[...] = jnp.dot(u_ref[...], xt_ref[...], preferred_element_type=F32)

    for c in range(n_chunk):
        lanes = slice(c * LANES, (c + 1) * LANES)

        def gate_body(jb, carry, c=c, lanes=lanes):
            j0 = pl.multiple_of(jb * jrows, jrows)
            gates = [jnp.zeros((jrows, LANES), F32) for _ in range(rows)]
            for h in range(PEER_HEADS):
                rank1 = rank_ref[h, c, pl.ds(j0, jrows), :]
                e1 = e1_ref[h, c, pl.ds(j0, jrows), :]
                for r in range(rows):
                    cnt = cnt_ref[h, c, pl.ds(r, 1), :]
                    e0 = e0_ref[h, c, pl.ds(r, 1), :]
                    gates[r] = gates[r] + jnp.where(rank1 < cnt, e1 * e0, 0.0)
            for r in range(rows):
                row0 = pl.multiple_of(r * N_SUBKEYS + j0, jrows)
                hh = ht_ref[pl.ds(row0, jrows), lanes]
                gelu = 0.5 * hh * (1.0 + lax.erf(hh * (2.0 ** -0.5)))
                act_ref[pl.ds(row0, jrows), lanes] = (gates[r] * gelu).astype(BF16)
            return carry

        lax.fori_loop(0, N_SUBKEYS // jrows, gate_body, 0)

    acc_ref[...] += jnp.dot(vt_ref[...], act_ref[...],
                            preferred_element_type=F32)

    @pl.when(ei == pl.num_programs(1) - 1)
    def _():
        ffn = acc_ref[...].T
        o_ref[...] = _layer_norm(DEEPNORM_ALPHA * x_ref[...] + ffn, g_ref[...], b_ref[...])


def _peer_dense(x1, u_b, vt_b, sel, g, b, tm=512, te=1024):
    n_tok = x1.shape[0]
    n_chunk = tm // LANES
    sel_spec = pl.BlockSpec((PEER_HEADS, n_chunk, N_SUBKEYS, LANES), lambda i, e: (0, i, 0, 0))
    row_spec = pl.BlockSpec((PEER_HEADS, n_chunk, te // N_SUBKEYS, LANES),
                            lambda i, e: (0, i, e, 0))
    vec_spec = pl.BlockSpec((1, D_MODEL), lambda i, e: (0, 0))
    return pl.pallas_call(
        functools.partial(_peer_dense_kernel, tm=tm, te=te),
        grid=(n_tok // tm, N_EXPERTS // te),
        in_specs=[pl.BlockSpec((tm, D_MODEL), lambda i, e: (i, 0)),
                  pl.BlockSpec((te, D_MODEL), lambda i, e: (e, 0)),
                  pl.BlockSpec((D_MODEL, te), lambda i, e: (0, e)),
                  row_spec, row_spec, sel_spec, sel_spec, vec_spec, vec_spec],
        out_specs=pl.BlockSpec((tm, D_MODEL), lambda i, e: (i, 0)),
        out_shape=jax.ShapeDtypeStruct((n_tok, D_MODEL), F32),
        scratch_shapes=[pltpu.VMEM((D_MODEL, tm), F32), pltpu.VMEM((D_MODEL, tm), BF16),
                        pltpu.VMEM((te, tm), F32), pltpu.VMEM((te, tm), BF16)],
        compiler_params=_cparams("parallel", "arbitrary"),
        name="peer_dense",
    )(x1, u_b, vt_b, *sel, g.reshape(1, D_MODEL), b.reshape(1, D_MODEL))


def _layer(x, w_in, w_out, ln1_g, ln1_b, peer_wq, sub_keys, peer_u, peer_v, ln2_g, ln2_b):
    batch, seq_len, _ = x.shape
    n_tok = batch * seq_len
    x2 = x.reshape(n_tok, D_MODEL)

    qkv, qkv_r4, qkv_r16 = _qkv_proj(x2, w_in.astype(BF16), batch, seq_len)
    qkv3 = qkv.reshape(batch, seq_len, QKV_WIDTH)
    o1, l1 = _dilated_branch(qkv3.reshape(batch, 1, seq_len, QKV_WIDTH), DILATIONS[0])
    dil_outs = [(o1.reshape(n_tok, GROUP_WIDTH), l1.reshape(n_tok, GROUP_WIDTH)),
                _dilated_branch(qkv_r4, DILATIONS[1]),
                _dilated_branch(qkv_r16, DILATIONS[2])]
    sb_out = _stick_breaking(qkv3)
    x1, x1b = _outproj_ln(dil_outs, sb_out, x2, w_out.astype(BF16), ln1_g, ln1_b, seq_len)

    keys_b = sub_keys.reshape(2 * PEER_HEADS, N_SUBKEYS, N_SUBKEYS).astype(BF16)
    sel = _peer_select(x1b, peer_wq.astype(BF16), keys_b)
    out = _peer_dense(x1, peer_u.astype(BF16), peer_v.astype(BF16).T, sel, ln2_g, ln2_b)
    return out.reshape(batch, seq_len, D_MODEL)


def kernel(x, w_in, w_out, ln1_g, ln1_b, peer_wq, peer_sub_keys, peer_u, peer_v, ln2_g, ln2_b):
    depth = w_in.shape[0]
    for layer in range(depth):
        x = _layer(x, w_in[layer], w_out[layer], ln1_g[layer], ln1_b[layer],
                   peer_wq[layer], peer_sub_keys[layer], peer_u[layer], peer_v[layer],
                   ln2_g[layer], ln2_b[layer])
    return x
```

```python
import functools

import jax
import jax.numpy as jnp
from jax import lax
from jax.experimental import pallas as pl
from jax.experimental.pallas import tpu as pltpu

F32 = jnp.float32
BF16 = jnp.bfloat16

D_MODEL = 1024
HEAD_DIM = 64
GROUP_WIDTH = 512
QKV_WIDTH = 6 * GROUP_WIDTH
DILATIONS = (1, 4, 16)
WINDOW_STEPS = 128
ROPE_DIM = HEAD_DIM // 4
ROPE_THETA = 500000.0
PEER_HEADS = 8
N_SUBKEYS = 128
N_EXPERTS = N_SUBKEYS * N_SUBKEYS
PEER_TOPK = 16
LN_EPS = 1e-5
DEEPNORM_ALPHA = 2.0 ** 0.25
LANES = 128
MXU_COLS = 256
NEG_BIG = -1e30
GATE_TILE_ROWS = 32
SB_EXP_UNDERFLOW = -104.0

VMEM_LIMIT = 56 * 1024 * 1024


def _cparams(*sem):
    return pltpu.CompilerParams(dimension_semantics=sem, vmem_limit_bytes=VMEM_LIMIT)


def _rope_tables(seq_len):
    half = ROPE_DIM // 2
    inv_freq = ROPE_THETA ** (-jnp.arange(half, dtype=F32) * 2.0 / ROPE_DIM)
    ang = jnp.arange(seq_len, dtype=F32)[:, None] * inv_freq[None, :]
    cos, sin = jnp.cos(ang), jnp.sin(ang)
    rest = HEAD_DIM - ROPE_DIM
    ones = jnp.ones((seq_len, rest), F32)
    zeros = jnp.zeros((seq_len, rest), F32)
    zh = jnp.zeros((seq_len, half), F32)
    cos_t = jnp.concatenate([cos, cos, ones], axis=1)
    up_t = jnp.concatenate([-sin, zh, zeros], axis=1)
    dn_t = jnp.concatenate([zh, sin, zeros], axis=1)
    rep = LANES // HEAD_DIM
    return tuple(jnp.tile(t, (1, rep)) for t in (cos_t, up_t, dn_t))


def _qkv_kernel(x_ref, w_ref, cos_ref, up_ref, dn_ref, o_ref, o4_ref, o16_ref, stage_ref):
    tm = x_ref.shape[0]
    xb = x_ref[...].astype(BF16)
    half = ROPE_DIM // 2
    for j in range(6):
        acc = jnp.dot(xb, w_ref[:, j * GROUP_WIDTH:(j + 1) * GROUP_WIDTH],
                      preferred_element_type=F32)
        for c in range(GROUP_WIDTH // LANES):
            a = acc[:, c * LANES:(c + 1) * LANES]
            if j in (0, 1):
                up = pltpu.roll(a, LANES - half, axis=1)
                dn = pltpu.roll(a, half, axis=1)
                a = a * cos_ref[...] + up * up_ref[...] + dn * dn_ref[...]
            if j in (0, 3):
                a = a * (HEAD_DIM ** -0.5)
            col = j * GROUP_WIDTH + c * LANES
            o_ref[:, col:col + LANES] = a.astype(BF16)
            if j < 3:
                stage_ref[col // LANES] = a
    for dil, od_ref in ((4, o4_ref), (16, o16_ref)):
        for r in range(dil):
            for cc in range(stage_ref.shape[0]):
                od_ref[r, :, cc * LANES:(cc + 1) * LANES] = (
                    stage_ref[cc, pl.ds(r, tm // dil, stride=dil), :].astype(BF16))


def _qkv_proj(x2, w_in_b, batch, seq_len, tm=512):
    n_tok = x2.shape[0]
    cos_t, up_t, dn_t = _rope_tables(seq_len)
    pos_blocks = seq_len // tm
    tab_spec = pl.BlockSpec((tm, LANES), lambda i: (i % pos_blocks, 0))
    a_width = 3 * GROUP_WIDTH

    def regrouped(dil):
        spec = pl.BlockSpec((None, dil, tm // dil, a_width),
                            lambda i: (i // pos_blocks, 0, i % pos_blocks, 0))
        sds = jax.ShapeDtypeStruct((batch, dil, seq_len // dil, a_width), BF16)
        return spec, sds

    spec4, sds4 = regrouped(4)
    spec16, sds16 = regrouped(16)
    return pl.pallas_call(
        _qkv_kernel,
        grid=(n_tok // tm,),
        in_specs=[pl.BlockSpec((tm, D_MODEL), lambda i: (i, 0)),
                  pl.BlockSpec((D_MODEL, QKV_WIDTH), lambda i: (0, 0)),
                  tab_spec, tab_spec, tab_spec],
        out_specs=[pl.BlockSpec((tm, QKV_WIDTH), lambda i: (i, 0)), spec4, spec16],
        out_shape=[jax.ShapeDtypeStruct((n_tok, QKV_WIDTH), BF16), sds4, sds16],
        scratch_shapes=[pltpu.VMEM((a_width // LANES, tm, LANES), F32)],
        compiler_params=_cparams("parallel"),
        name="qkv_proj",
    )(x2, w_in_b, cos_t, up_t, dn_t)


def _dil_kernel(q_ref, k_ref, v_ref, o_ref, lse_ref, *, length):
    tq = LANES
    tk = 2 * LANES
    lane = lax.broadcasted_iota(jnp.int32, (1, LANES), 1)
    head0 = lane < HEAD_DIM
    row = lax.broadcasted_iota(jnp.int32, (tq, tk), 0)
    col = lax.broadcasted_iota(jnp.int32, (tq, tk), 1)
    rel = row - col

    n_blocks = length // tq
    n_res = q_ref.shape[0]

    def body(idx, carry):
        res = idx // n_blocks
        q0 = pl.multiple_of((idx % n_blocks) * tq, tq)
        k0 = pl.multiple_of(jnp.maximum(q0 - tq, 0), tq)
        q = q_ref[res, pl.ds(q0, tq), :]
        k = k_ref[res, pl.ds(k0, tk), :]
        v = v_ref[res, pl.ds(k0, tk), :]
        diff = rel + (q0 - k0)
        valid = jnp.where(diff >= 0, diff, WINDOW_STEPS + 1) <= WINDOW_STEPS
        outs, lses = [], []
        for h in range(2):
            hm = head0 if h == 0 else jnp.logical_not(head0)
            qh = jnp.where(hm, q, jnp.zeros_like(q))
            s = lax.dot_general(qh, k, (((1,), (1,)), ((), ())),
                                preferred_element_type=F32)
            s = jnp.where(valid, s, NEG_BIG)
            m = jnp.max(s, axis=1, keepdims=True)
            p = jnp.exp(s - m)
            l = jnp.sum(p, axis=1, keepdims=True)
            oh = jnp.dot(p.astype(BF16), v, preferred_element_type=F32)
            outs.append(oh / l)
            lses.append(m + jnp.log(l))
        o_ref[res, pl.ds(q0, tq), :] = jnp.where(head0, outs[0], outs[1])
        lse_ref[res, pl.ds(q0, tq), :] = jnp.where(head0, lses[0], lses[1])
        return carry

    lax.fori_loop(0, n_res * n_blocks, body, 0, unroll=4)


def _dilated_branch(qkv_regrouped, dil):
    batch, _, length, _ = qkv_regrouped.shape
    gcols = GROUP_WIDTH // LANES
    n_res = min(dil, max(1, (8 * LANES) // length))

    def in_spec(group):
        return pl.BlockSpec((None, n_res, length, LANES),
                            lambda b, r, hp: (b, r, 0, group * gcols + hp))

    out_spec = pl.BlockSpec((None, n_res, length, LANES), lambda b, r, hp: (b, r, 0, hp))
    out_sds = jax.ShapeDtypeStruct((batch, dil, length, GROUP_WIDTH), F32)
    return pl.pallas_call(
        functools.partial(_dil_kernel, length=length),
        grid=(batch, dil // n_res, gcols),
        in_specs=[in_spec(0), in_spec(1), in_spec(2)],
        out_specs=[out_spec, out_spec],
        out_shape=[out_sds, out_sds],
        compiler_params=_cparams("parallel", "parallel", "parallel"),
        name=f"dilated_attn_d{dil}",
    )(qkv_regrouped, qkv_regrouped, qkv_regrouped)


def _sb_kernel(q_ref, k_ref, v_ref, o_ref, acc_ref, run_ref, *, seq_len, blk):
    lane = lax.broadcasted_iota(jnp.int32, (1, LANES), 1)
    head0 = lane < HEAD_DIM
    row = lax.broadcasted_iota(jnp.int32, (blk, blk), 0)
    col = lax.broadcasted_iota(jnp.int32, (blk, blk), 1)
    below = row > col
    tri = jnp.where(below, 1.0, 0.0).astype(BF16)
    tri2 = jnp.concatenate([tri, tri], axis=0)
    reps = blk // LANES

    def q_body(qi, carry):
        q0 = pl.multiple_of(qi * blk, blk)
        q = q_ref[pl.ds(q0, blk), :]
        qh = [jnp.where(head0, q, jnp.zeros_like(q)), jnp.where(head0, jnp.zeros_like(q), q)]
        acc_ref[...] = jnp.zeros_like(acc_ref)
        run_ref[...] = jnp.zeros_like(run_ref)

        def blocks(sweep):
            kv = []
            for kj, diag in sweep:
                k0 = pl.multiple_of(kj * blk, blk)
                kv.append((k_ref[pl.ds(k0, blk), :], v_ref[pl.ds(k0, blk), :], diag))
            tops = []
            for h in range(2):
                parts = []
                for k, v, diag in kv:
                    z = lax.dot_general(qh[h], k, (((1,), (1,)), ((), ())),
                                        preferred_element_type=F32)
                    sp = jnp.maximum(z, 0.0) + jnp.log1p(jnp.exp(-jnp.abs(z)))
                    log_1m = -sp
                    log_b = z - sp
                    if diag:
                        log_1m = jnp.where(below, log_1m, 0.0)
                    hi = log_1m.astype(BF16)
                    lo = (log_1m - hi.astype(F32)).astype(BF16)
                    later = jnp.dot(jnp.concatenate([hi, lo], axis=1), tri2,
                                    preferred_element_type=F32)
                    parts.append((log_b + later, jnp.sum(log_1m, axis=1, keepdims=True), v, diag))
                run = run_ref[h]
                for pre, row_sum, v, diag in parts:
                    run_w = jnp.concatenate([run] * reps, axis=1)
                    a = jnp.exp(pre + run_w)
                    if diag:
                        a = jnp.where(below, a, 0.0)
                    acc_ref[h] += jnp.dot(a.astype(BF16), v, preferred_element_type=F32)
                    run = run + row_sum
                run_ref[h] = run
                tops.append(jnp.max(run))
            return jnp.maximum(tops[0], tops[1])

        top = lax.cond(qi > 0,
                       lambda: blocks([(qi, True), (qi - 1, False)]),
                       lambda: blocks([(qi, True)]))

        def k_cond(c):
            return jnp.logical_and(c[0] >= 0, c[1] >= SB_EXP_UNDERFLOW)

        def k_body(c):
            return c[0] - 1, blocks([(c[0], False)])

        lax.while_loop(k_cond, k_body, (qi - 2, top))
        o_ref[pl.ds(q0, blk), :] = jnp.where(head0, acc_ref[0], acc_ref[1]).astype(o_ref.dtype)
        return carry

    lax.fori_loop(0, seq_len // blk, q_body, 0)


def _stick_breaking(qkv3, blk=256):
    batch, seq_len, _ = qkv3.shape
    gcols = GROUP_WIDTH // LANES

    def in_spec(group):
        return pl.BlockSpec((None, seq_len, LANES),
                            lambda b, hp: (b, 0, (3 + group) * gcols + hp))

    out = pl.pallas_call(
        functools.partial(_sb_kernel, seq_len=seq_len, blk=blk),
        grid=(batch, gcols),
        in_specs=[in_spec(0), in_spec(1), in_spec(2)],
        out_specs=pl.BlockSpec((None, seq_len, LANES), lambda b, hp: (b, 0, hp)),
        out_shape=jax.ShapeDtypeStruct((batch, seq_len, GROUP_WIDTH), BF16),
        scratch_shapes=[pltpu.VMEM((2, blk, LANES), F32), pltpu.VMEM((2, blk, LANES), F32)],
        compiler_params=_cparams("parallel", "parallel"),
        name="stick_breaking_attn",
    )(qkv3, qkv3, qkv3)
    return out.reshape(batch * seq_len, GROUP_WIDTH)


def _layer_norm(y, g, b):
    mu = jnp.mean(y, axis=-1, keepdims=True)
    yc = y - mu
    var = jnp.mean(yc * yc, axis=-1, keepdims=True)
    return yc * lax.rsqrt(var + LN_EPS) * g + b


def _outproj_kernel(o1, o2, o3, l1, l2, l3, sb_ref, x_ref, w_ref, g_ref, b_ref,
                    y_ref, yb_ref, o2_scr, o3_scr, l2_scr, l3_scr):
    tm = x_ref.shape[0]
    n_col = GROUP_WIDTH // LANES
    for src, dst in ((o2, o2_scr), (l2, l2_scr), (o3, o3_scr), (l3, l3_scr)):
        dil = src.shape[0]
        for r in range(dil):
            for cc in range(n_col):
                dst[cc, pl.ds(r, tm // dil, stride=dil), :] = src[r, :, cc * LANES:(cc + 1) * LANES]
    parts = []
    for cc in range(n_col):
        cols = slice(cc * LANES, (cc + 1) * LANES)
        la, lb, lc = l1[:, cols], l2_scr[cc], l3_scr[cc]
        mx = jnp.maximum(jnp.maximum(la, lb), lc)
        wa, wb, wc = jnp.exp(la - mx), jnp.exp(lb - mx), jnp.exp(lc - mx)
        parts.append((wa * o1[:, cols] + wb * o2_scr[cc] + wc * o3_scr[cc]) / (wa + wb + wc))
    oa = jnp.concatenate(parts, axis=1)
    mix = jnp.dot(oa.astype(BF16), w_ref[:GROUP_WIDTH, :], preferred_element_type=F32)
    mix = mix + jnp.dot(sb_ref[...], w_ref[GROUP_WIDTH:, :], preferred_element_type=F32)
    y = _layer_norm(DEEPNORM_ALPHA * x_ref[...] + mix, g_ref[...], b_ref[...])
    y_ref[...] = y
    yb_ref[...] = y.astype(BF16)


def _outproj_ln(dil_outs, sb_out, x2, w_out_b, g, b, seq_len, tm=512):
    n_tok = x2.shape[0]
    pos_blocks = seq_len // tm
    half_spec = pl.BlockSpec((tm, GROUP_WIDTH), lambda i: (i, 0))
    full_spec = pl.BlockSpec((tm, D_MODEL), lambda i: (i, 0))
    vec_spec = pl.BlockSpec((1, D_MODEL), lambda i: (0, 0))

    def grouped_spec(dil):
        return pl.BlockSpec((None, dil, tm // dil, GROUP_WIDTH),
                            lambda i: (i // pos_blocks, 0, i % pos_blocks, 0))

    (o1, l1), (o2, l2), (o3, l3) = dil_outs
    s2, s3 = grouped_spec(o2.shape[1]), grouped_spec(o3.shape[1])
    return pl.pallas_call(
        _outproj_kernel,
        grid=(n_tok // tm,),
        in_specs=[half_spec, s2, s3, half_spec, s2, s3, half_spec, full_spec,
                  pl.BlockSpec((D_MODEL, D_MODEL), lambda i: (0, 0)),
                  vec_spec, vec_spec],
        out_specs=[full_spec, full_spec],
        out_shape=[jax.ShapeDtypeStruct((n_tok, D_MODEL), F32),
                   jax.ShapeDtypeStruct((n_tok, D_MODEL), BF16)],
        scratch_shapes=[pltpu.VMEM((GROUP_WIDTH // LANES, tm, LANES), F32)] * 4,
        compiler_params=_cparams("parallel"),
        name="outproj_ln1",
    )(o1, o2, o3, l1, l2, l3, sb_out, x2, w_out_b, g.reshape(1, D_MODEL), b.reshape(1, D_MODEL))


def _staircase():
    return [(a, b) for a in range(PEER_TOPK) for b in range(PEER_TOPK)
            if (a + 1) * (b + 1) <= PEER_TOPK]


def _peer_select_kernel(x_ref, wq_ref, keys_ref, cnt_out, e0_out, rank_out, e1_out,
                        q_scr, sc_scr, rank_scr, top_scr, cnt_scr, rz_scr, *, tm):
    n_hp = 2 * PEER_HEADS
    n_chunk = tm // LANES
    q_scr[...] = jnp.dot(x_ref[...], wq_ref[...], preferred_element_type=F32).astype(BF16)
    for hp in range(n_hp):
        sc = lax.dot_general(
            keys_ref[hp], q_scr[:, hp * N_SUBKEYS:(hp + 1) * N_SUBKEYS],
            (((1,), (1,)), ((), ())), preferred_element_type=F32)
        for c in range(n_chunk):
            sc_scr[hp, c] = sc[:, c * LANES:(c + 1) * LANES]

    key_id = lax.broadcasted_iota(jnp.int32, (N_SUBKEYS, LANES), 0)

    def stage1(idx, carry):
        head = idx // n_chunk
        c = idx % n_chunk
        hps = (2 * head, 2 * head + 1)
        s = [sc_scr[hp, c] for hp in hps]
        rank = [jnp.full((N_SUBKEYS, LANES), float(PEER_TOPK), F32) for _ in hps]
        for it in range(PEER_TOPK):
            for p, hp in enumerate(hps):
                m = jnp.max(s[p], axis=0, keepdims=True)
                first = jnp.min(jnp.where(s[p] == m, key_id, N_SUBKEYS), axis=0, keepdims=True)
                sel = key_id == first
                rank[p] = jnp.where(sel, float(it), rank[p])
                s[p] = jnp.where(sel, -jnp.inf, s[p])
                top_scr[hp, c, pl.ds(it, 1), :] = m
        for p, hp in enumerate(hps):
            rank_scr[hp, c] = rank[p]
        return carry

    lax.fori_loop(0, PEER_HEADS * n_chunk, stage1, 0, unroll=2)

    pairs = _staircase()

    def stage2(c, carry):
        def heads_on_sublanes(half, a):
            return jnp.concatenate(
                [top_scr[2 * h + half, c, pl.ds(a, 1), :] for h in range(PEER_HEADS)], axis=0)

        t0 = [heads_on_sublanes(0, a) for a in range(PEER_TOPK)]
        t1 = [heads_on_sublanes(1, b) for b in range(PEER_TOPK)]
        cand = [t0[a] + t1[b] for a, b in pairs]
        best = cand[0]
        zsum = jnp.zeros((PEER_HEADS, LANES), F32)

        def tree(op, vals):
            while len(vals) > 1:
                vals = [op(vals[i], vals[i + 1]) if i + 1 < len(vals) else vals[i]
                        for i in range(0, len(vals), 2)]
            return vals[0]

        for it in range(PEER_TOPK):
            m = tree(jnp.maximum, cand)
            zsum = zsum + jnp.exp(m - best)
            first = tree(jnp.minimum, [jnp.where(cand[k] == m, float(k), float(len(pairs)))
                                       for k in range(len(pairs))])
            cand = [jnp.where(first == float(k), -jnp.inf, cand[k]) for k in range(len(pairs))]
        for a in range(PEER_TOPK):
            taken = [jnp.where(cand[k] == -jnp.inf, 1.0, 0.0)
                     for k, (pa, _) in enumerate(pairs) if pa == a]
            cnt_scr[c, a] = tree(jnp.add, taken)
        rz_scr[c] = 0.5 / zsum
        return carry

    lax.fori_loop(0, n_chunk, stage2, 0, unroll=2)

    def stage3(c, carry):
        for head in range(PEER_HEADS):
            rank0 = rank_scr[2 * head, c]
            cntk = jnp.zeros((N_SUBKEYS, LANES), F32)
            for a in range(PEER_TOPK):
                ca = cnt_scr[c, a, pl.ds(head, 1), :]
                cntk = jnp.where(rank0 == float(a), ca, cntk)
            m0 = top_scr[2 * head, c, pl.ds(0, 1), :]
            m1 = top_scr[2 * head + 1, c, pl.ds(0, 1), :]
            rz = rz_scr[c, pl.ds(head, 1), :]
            cnt_out[head, c] = cntk
            e0_out[head, c] = jnp.exp(sc_scr[2 * head, c] - m0) * rz
            rank_out[head, c] = rank_scr[2 * head + 1, c]
            e1_out[head, c] = jnp.exp(sc_scr[2 * head + 1, c] - m1)
        return carry

    lax.fori_loop(0, n_chunk, stage3, 0)


def _peer_select(x1b, wq_b, keys_b, tm=256):
    n_tok = x1b.shape[0]
    n_hp = 2 * PEER_HEADS
    qdim = n_hp * N_SUBKEYS
    n_chunk = tm // LANES
    out_spec = pl.BlockSpec((PEER_HEADS, n_chunk, N_SUBKEYS, LANES), lambda i: (0, i, 0, 0))
    out_sds = jax.ShapeDtypeStruct((PEER_HEADS, n_tok // LANES, N_SUBKEYS, LANES), F32)
    return pl.pallas_call(
        functools.partial(_peer_select_kernel, tm=tm),
        grid=(n_tok // tm,),
        in_specs=[pl.BlockSpec((tm, D_MODEL), lambda i: (i, 0)),
                  pl.BlockSpec((D_MODEL, qdim), lambda i: (0, 0)),
                  pl.BlockSpec((n_hp, N_SUBKEYS, N_SUBKEYS), lambda i: (0, 0, 0))],
        out_specs=[out_spec] * 4,
        out_shape=[out_sds] * 4,
        scratch_shapes=[pltpu.VMEM((tm, qdim), BF16),
                        pltpu.VMEM((n_hp, n_chunk, N_SUBKEYS, LANES), F32),
                        pltpu.VMEM((n_hp, n_chunk, N_SUBKEYS, LANES), F32),
                        pltpu.VMEM((n_hp, n_chunk, PEER_TOPK, LANES), F32),
                        pltpu.VMEM((n_chunk, PEER_TOPK, PEER_HEADS, LANES), F32),
                        pltpu.VMEM((n_chunk, PEER_HEADS, LANES), F32)],
        compiler_params=_cparams("parallel"),
        name="peer_select",
    )(x1b, wq_b, keys_b)


def _peer_dense_kernel(x_ref, u_ref, vt_ref, cnt_ref, e0_ref, rank_ref, e1_ref,
                       g_ref, b_ref, o_ref, acc_ref, xt_ref, ht_ref, act_ref, *, tm, te):
    ei = pl.program_id(1)
    n_chunk = tm // LANES
    rows = te // N_SUBKEYS
    jrows = GATE_TILE_ROWS

    @pl.when(ei == 0)
    def _():
        acc_ref[...] = jnp.zeros_like(acc_ref)
        xt_ref[...] = x_ref[...].T.astype(BF16)

    ht_ref[...] = jnp.dot(u_ref[...], xt_ref[...], preferred_element_type=F32)

    for c in range(n_chunk):
        lanes = slice(c * LANES, (c + 1) * LANES)

        def gate_body(jb, carry, c=c, lanes=lanes):
            j0 = pl.multiple_of(jb * jrows, jrows)
            gates = [jnp.zeros((jrows, LANES), F32) for _ in range(rows)]
            for h in range(PEER_HEADS):
                rank1 = rank_ref[h, c, pl.ds(j0, jrows), :]
                e1 = e1_ref[h, c, pl.ds(j0, jrows), :]
                for r in range(rows):
                    cnt = cnt_ref[h, c, pl.ds(r, 1), :]
                    e0 = e0_ref[h, c, pl.ds(r, 1), :]
                    gates[r] = gates[r] + jnp.where(rank1 < cnt, e1 * e0, 0.0)
            for r in range(rows):
                row0 = pl.multiple_of(r * N_SUBKEYS + j0, jrows)
                hh = ht_ref[pl.ds(row0, jrows), lanes]
                gelu2 = hh * (1.0 + lax.erf(hh * (2.0 ** -0.5)))
                act_ref[pl.ds(row0, jrows), lanes] = (gates[r] * gelu2).astype(BF16)
            return carry

        lax.fori_loop(0, N_SUBKEYS // jrows, gate_body, 0)

    acc_ref[...] += jnp.dot(vt_ref[...], act_ref[...],
                            preferred_element_type=F32)

    @pl.when(ei == pl.num_programs(1) - 1)
    def _():
        ffn = acc_ref[...].T
        o_ref[...] = _layer_norm(DEEPNORM_ALPHA * x_ref[...] + ffn, g_ref[...], b_ref[...])


def _peer_dense(x1, u_b, vt_b, sel, g, b, tm=512, te=1024):
    n_tok = x1.shape[0]
    n_chunk = tm // LANES
    sel_spec = pl.BlockSpec((PEER_HEADS, n_chunk, N_SUBKEYS, LANES), lambda i, e: (0, i, 0, 0))
    row_spec = pl.BlockSpec((PEER_HEADS, n_chunk, te // N_SUBKEYS, LANES),
                            lambda i, e: (0, i, e, 0))
    vec_spec = pl.BlockSpec((1, D_MODEL), lambda i, e: (0, 0))
    return pl.pallas_call(
        functools.partial(_peer_dense_kernel, tm=tm, te=te),
        grid=(n_tok // tm, N_EXPERTS // te),
        in_specs=[pl.BlockSpec((tm, D_MODEL), lambda i, e: (i, 0)),
                  pl.BlockSpec((te, D_MODEL), lambda i, e: (e, 0)),
                  pl.BlockSpec((D_MODEL, te), lambda i, e: (0, e)),
                  row_spec, row_spec, sel_spec, sel_spec, vec_spec, vec_spec],
        out_specs=pl.BlockSpec((tm, D_MODEL), lambda i, e: (i, 0)),
        out_shape=jax.ShapeDtypeStruct((n_tok, D_MODEL), F32),
        scratch_shapes=[pltpu.VMEM((D_MODEL, tm), F32), pltpu.VMEM((D_MODEL, tm), BF16),
                        pltpu.VMEM((te, tm), F32), pltpu.VMEM((te, tm), BF16)],
        compiler_params=_cparams("parallel", "arbitrary"),
        name="peer_dense",
    )(x1, u_b, vt_b, *sel, g.reshape(1, D_MODEL), b.reshape(1, D_MODEL))


def _layer(x, w_in, w_out, ln1_g, ln1_b, peer_wq, sub_keys, peer_u, peer_v, ln2_g, ln2_b):
    batch, seq_len, _ = x.shape
    n_tok = batch * seq_len
    x2 = x.reshape(n_tok, D_MODEL)

    qkv, qkv_r4, qkv_r16 = _qkv_proj(x2, w_in.astype(BF16), batch, seq_len)
    qkv3 = qkv.reshape(batch, seq_len, QKV_WIDTH)
    o1, l1 = _dilated_branch(qkv3.reshape(batch, 1, seq_len, QKV_WIDTH), DILATIONS[0])
    dil_outs = [(o1.reshape(n_tok, GROUP_WIDTH), l1.reshape(n_tok, GROUP_WIDTH)),
                _dilated_branch(qkv_r4, DILATIONS[1]),
                _dilated_branch(qkv_r16, DILATIONS[2])]
    sb_out = _stick_breaking(qkv3)
    x1, x1b = _outproj_ln(dil_outs, sb_out, x2, w_out.astype(BF16), ln1_g, ln1_b, seq_len)

    keys_b = sub_keys.reshape(2 * PEER_HEADS, N_SUBKEYS, N_SUBKEYS).astype(BF16)
    sel = _peer_select(x1b, peer_wq.astype(BF16), keys_b)
    out = _peer_dense(x1, peer_u.astype(BF16), peer_v.astype(BF16).T, sel, ln2_g, ln2_b)
    return out.reshape(batch, seq_len, D_MODEL)


def kernel(x, w_in, w_out, ln1_g, ln1_b, peer_wq, peer_sub_keys, peer_u, peer_v, ln2_g, ln2_b):
    depth = w_in.shape[0]
    for layer in range(depth):
        x = _layer(x, w_in[layer], w_out[layer], ln1_g[layer], ln1_b[layer],
                   peer_wq[layer], peer_sub_keys[layer], peer_u[layer], peer_v[layer],
                   ln2_g[layer], ln2_b[layer])
    return x
```

```python
import functools

import jax
import jax.numpy as jnp
from jax import lax
from jax.experimental import pallas as pl
from jax.experimental.pallas import tpu as pltpu

F32 = jnp.float32
BF16 = jnp.bfloat16

D_MODEL = 1024
HEAD_DIM = 64
GROUP_WIDTH = 512
QKV_WIDTH = 6 * GROUP_WIDTH
DILATIONS = (1, 4, 16)
WINDOW_STEPS = 128
ROPE_DIM = HEAD_DIM // 4
ROPE_THETA = 500000.0
PEER_HEADS = 8
N_SUBKEYS = 128
N_EXPERTS = N_SUBKEYS * N_SUBKEYS
PEER_TOPK = 16
LN_EPS = 1e-5
DEEPNORM_ALPHA = 2.0 ** 0.25
LANES = 128
MXU_COLS = 256
NEG_BIG = -1e30
GATE_TILE_KEYS = 8
GATE_TILE_ROWS = 32
SB_EXP_UNDERFLOW = -104.0

VMEM_LIMIT = 56 * 1024 * 1024


def _cparams(*sem):
    return pltpu.CompilerParams(dimension_semantics=sem, vmem_limit_bytes=VMEM_LIMIT)


def _rope_tables(seq_len):
    half = ROPE_DIM // 2
    inv_freq = ROPE_THETA ** (-jnp.arange(half, dtype=F32) * 2.0 / ROPE_DIM)
    ang = jnp.arange(seq_len, dtype=F32)[:, None] * inv_freq[None, :]
    cos, sin = jnp.cos(ang), jnp.sin(ang)
    rest = HEAD_DIM - ROPE_DIM
    ones = jnp.ones((seq_len, rest), F32)
    zeros = jnp.zeros((seq_len, rest), F32)
    zh = jnp.zeros((seq_len, half), F32)
    cos_t = jnp.concatenate([cos, cos, ones], axis=1)
    up_t = jnp.concatenate([-sin, zh, zeros], axis=1)
    dn_t = jnp.concatenate([zh, sin, zeros], axis=1)
    rep = LANES // HEAD_DIM
    return tuple(jnp.tile(t, (1, rep)) for t in (cos_t, up_t, dn_t))


def _qkv_kernel(x_ref, w_ref, cos_ref, up_ref, dn_ref, o_ref, o4_ref, o16_ref, stage_ref):
    tm = x_ref.shape[0]
    xb = x_ref[...].astype(BF16)
    half = ROPE_DIM // 2
    for j in range(6):
        acc = jnp.dot(xb, w_ref[:, j * GROUP_WIDTH:(j + 1) * GROUP_WIDTH],
                      preferred_element_type=F32)
        for c in range(GROUP_WIDTH // LANES):
            a = acc[:, c * LANES:(c + 1) * LANES]
            if j in (0, 1):
                up = pltpu.roll(a, LANES - half, axis=1)
                dn = pltpu.roll(a, half, axis=1)
                a = a * cos_ref[...] + up * up_ref[...] + dn * dn_ref[...]
            if j in (0, 3):
                a = a * (HEAD_DIM ** -0.5)
            col = j * GROUP_WIDTH + c * LANES
            o_ref[:, col:col + LANES] = a.astype(BF16)
            if j < 3:
                stage_ref[col // LANES] = a
    for dil, od_ref in ((4, o4_ref), (16, o16_ref)):
        for r in range(dil):
            for cc in range(stage_ref.shape[0]):
                od_ref[r, :, cc * LANES:(cc + 1) * LANES] = (
                    stage_ref[cc, pl.ds(r, tm // dil, stride=dil), :].astype(BF16))


def _qkv_proj(x2, w_in_b, batch, seq_len, tm=512):
    n_tok = x2.shape[0]
    cos_t, up_t, dn_t = _rope_tables(seq_len)
    pos_blocks = seq_len // tm
    tab_spec = pl.BlockSpec((tm, LANES), lambda i: (i % pos_blocks, 0))
    a_width = 3 * GROUP_WIDTH

    def regrouped(dil):
        spec = pl.BlockSpec((None, dil, tm // dil, a_width),
                            lambda i: (i // pos_blocks, 0, i % pos_blocks, 0))
        sds = jax.ShapeDtypeStruct((batch, dil, seq_len // dil, a_width), BF16)
        return spec, sds

    spec4, sds4 = regrouped(4)
    spec16, sds16 = regrouped(16)
    return pl.pallas_call(
        _qkv_kernel,
        grid=(n_tok // tm,),
        in_specs=[pl.BlockSpec((tm, D_MODEL), lambda i: (i, 0)),
                  pl.BlockSpec((D_MODEL, QKV_WIDTH), lambda i: (0, 0)),
                  tab_spec, tab_spec, tab_spec],
        out_specs=[pl.BlockSpec((tm, QKV_WIDTH), lambda i: (i, 0)), spec4, spec16],
        out_shape=[jax.ShapeDtypeStruct((n_tok, QKV_WIDTH), BF16), sds4, sds16],
        scratch_shapes=[pltpu.VMEM((a_width // LANES, tm, LANES), F32)],
        compiler_params=_cparams("parallel"),
        name="qkv_proj",
    )(x2, w_in_b, cos_t, up_t, dn_t)


def _dil_kernel(q_ref, k_ref, v_ref, o_ref, lse_ref, *, length):
    tq = LANES
    tk = 2 * LANES
    lane = lax.broadcasted_iota(jnp.int32, (1, LANES), 1)
    head0 = lane < HEAD_DIM
    row = lax.broadcasted_iota(jnp.int32, (tq, tk), 0)
    col = lax.broadcasted_iota(jnp.int32, (tq, tk), 1)
    rel = row - col

    n_blocks = length // tq
    n_res = q_ref.shape[0]

    def body(idx, carry):
        res = idx // n_blocks
        q0 = pl.multiple_of((idx % n_blocks) * tq, tq)
        k0 = pl.multiple_of(jnp.maximum(q0 - tq, 0), tq)
        q = q_ref[res, pl.ds(q0, tq), :]
        k = k_ref[res, pl.ds(k0, tk), :]
        v = v_ref[res, pl.ds(k0, tk), :]
        diff = rel + (q0 - k0)
        valid = jnp.where(diff >= 0, diff, WINDOW_STEPS + 1) <= WINDOW_STEPS
        outs, lses = [], []
        for h in range(2):
            hm = head0 if h == 0 else jnp.logical_not(head0)
            qh = jnp.where(hm, q, jnp.zeros_like(q))
            s = lax.dot_general(qh, k, (((1,), (1,)), ((), ())),
                                preferred_element_type=F32)
            s = jnp.where(valid, s, NEG_BIG)
            m = jnp.max(s, axis=1, keepdims=True)
            p = jnp.exp(s - m)
            l = jnp.sum(p, axis=1, keepdims=True)
            oh = jnp.dot(p.astype(BF16), v, preferred_element_type=F32)
            outs.append(oh / l)
            lses.append(m + jnp.log(l))
        o_ref[res, pl.ds(q0, tq), :] = jnp.where(head0, outs[0], outs[1])
        lse_ref[res, pl.ds(q0, tq), :] = jnp.where(head0, lses[0], lses[1])
        return carry

    lax.fori_loop(0, n_res * n_blocks, body, 0, unroll=4)


def _dilated_branch(qkv_regrouped, dil):
    batch, _, length, _ = qkv_regrouped.shape
    gcols = GROUP_WIDTH // LANES
    n_res = min(dil, max(1, (8 * LANES) // length))

    def in_spec(group):
        return pl.BlockSpec((None, n_res, length, LANES),
                            lambda b, r, hp: (b, r, 0, group * gcols + hp))

    out_spec = pl.BlockSpec((None, n_res, length, LANES), lambda b, r, hp: (b, r, 0, hp))
    out_sds = jax.ShapeDtypeStruct((batch, dil, length, GROUP_WIDTH), F32)
    return pl.pallas_call(
        functools.partial(_dil_kernel, length=length),
        grid=(batch, dil // n_res, gcols),
        in_specs=[in_spec(0), in_spec(1), in_spec(2)],
        out_specs=[out_spec, out_spec],
        out_shape=[out_sds, out_sds],
        compiler_params=_cparams("parallel", "parallel", "parallel"),
        name=f"dilated_attn_d{dil}",
    )(qkv_regrouped, qkv_regrouped, qkv_regrouped)


def _sb_kernel(q_ref, k_ref, v_ref, o_ref, acc_ref, run_ref, *, seq_len, blk):
    lane = lax.broadcasted_iota(jnp.int32, (1, LANES), 1)
    head0 = lane < HEAD_DIM
    row = lax.broadcasted_iota(jnp.int32, (blk, blk), 0)
    col = lax.broadcasted_iota(jnp.int32, (blk, blk), 1)
    below = row > col
    tri = jnp.where(below, 1.0, 0.0).astype(BF16)
    tri2 = jnp.concatenate([tri, tri], axis=0)
    reps = blk // LANES

    def q_body(qi, carry):
        q0 = pl.multiple_of(qi * blk, blk)
        q = q_ref[pl.ds(q0, blk), :]
        qh = [jnp.where(head0, q, jnp.zeros_like(q)), jnp.where(head0, jnp.zeros_like(q), q)]
        acc_ref[...] = jnp.zeros_like(acc_ref)
        run_ref[...] = jnp.zeros_like(run_ref)

        def blocks(sweep):
            kv = []
            for kj, diag in sweep:
                k0 = pl.multiple_of(kj * blk, blk)
                kv.append((k_ref[pl.ds(k0, blk), :], v_ref[pl.ds(k0, blk), :], diag))
            tops = []
            for h in range(2):
                parts = []
                for k, v, diag in kv:
                    z = lax.dot_general(qh[h], k, (((1,), (1,)), ((), ())),
                                        preferred_element_type=F32)
                    sp = jnp.maximum(z, 0.0) + jnp.log1p(jnp.exp(-jnp.abs(z)))
                    log_1m = -sp
                    log_b = z - sp
                    if diag:
                        log_1m = jnp.where(below, log_1m, 0.0)
                    hi = log_1m.astype(BF16)
                    lo = (log_1m - hi.astype(F32)).astype(BF16)
                    later = jnp.dot(jnp.concatenate([hi, lo], axis=1), tri2,
                                    preferred_element_type=F32)
                    parts.append((log_b + later, jnp.sum(log_1m, axis=1, keepdims=True), v, diag))
                run = run_ref[h]
                for pre, row_sum, v, diag in parts:
                    run_w = jnp.concatenate([run] * reps, axis=1)
                    a = jnp.exp(pre + run_w)
                    if diag:
                        a = jnp.where(below, a, 0.0)
                    acc_ref[h] += jnp.dot(a.astype(BF16), v, preferred_element_type=F32)
                    run = run + row_sum
                run_ref[h] = run
                tops.append(jnp.max(run))
            return jnp.maximum(tops[0], tops[1])

        top = lax.cond(qi > 0,
                       lambda: blocks([(qi, True), (qi - 1, False)]),
                       lambda: blocks([(qi, True)]))

        def k_cond(c):
            return jnp.logical_and(c[0] >= 0, c[1] >= SB_EXP_UNDERFLOW)

        def k_body(c):
            return c[0] - 1, blocks([(c[0], False)])

        lax.while_loop(k_cond, k_body, (qi - 2, top))
        o_ref[pl.ds(q0, blk), :] = jnp.where(head0, acc_ref[0], acc_ref[1]).astype(o_ref.dtype)
        return carry

    lax.fori_loop(0, seq_len // blk, q_body, 0)


def _stick_breaking(qkv3, blk=256):
    batch, seq_len, _ = qkv3.shape
    gcols = GROUP_WIDTH // LANES

    def in_spec(group):
        return pl.BlockSpec((None, seq_len, LANES),
                            lambda b, hp: (b, 0, (3 + group) * gcols + hp))

    out = pl.pallas_call(
        functools.partial(_sb_kernel, seq_len=seq_len, blk=blk),
        grid=(batch, gcols),
        in_specs=[in_spec(0), in_spec(1), in_spec(2)],
        out_specs=pl.BlockSpec((None, seq_len, LANES), lambda b, hp: (b, 0, hp)),
        out_shape=jax.ShapeDtypeStruct((batch, seq_len, GROUP_WIDTH), BF16),
        scratch_shapes=[pltpu.VMEM((2, blk, LANES), F32), pltpu.VMEM((2, blk, LANES), F32)],
        compiler_params=_cparams("parallel", "parallel"),
        name="stick_breaking_attn",
    )(qkv3, qkv3, qkv3)
    return out.reshape(batch * seq_len, GROUP_WIDTH)


def _layer_norm(y, g, b):
    mu = jnp.mean(y, axis=-1, keepdims=True)
    yc = y - mu
    var = jnp.mean(yc * yc, axis=-1, keepdims=True)
    return yc * lax.rsqrt(var + LN_EPS) * g + b


def _outproj_kernel(o1, o2, o3, l1, l2, l3, sb_ref, x_ref, w_ref, g_ref, b_ref,
                    y_ref, yb_ref, o2_scr, o3_scr, l2_scr, l3_scr):
    tm = x_ref.shape[0]
    n_col = GROUP_WIDTH // LANES
    for src, dst in ((o2, o2_scr), (l2, l2_scr), (o3, o3_scr), (l3, l3_scr)):
        dil = src.shape[0]
        for r in range(dil):
            for cc in range(n_col):
                dst[cc, pl.ds(r, tm // dil, stride=dil), :] = src[r, :, cc * LANES:(cc + 1) * LANES]
    parts = []
    for cc in range(n_col):
        cols = slice(cc * LANES, (cc + 1) * LANES)
        la, lb, lc = l1[:, cols], l2_scr[cc], l3_scr[cc]
        mx = jnp.maximum(jnp.maximum(la, lb), lc)
        wa, wb, wc = jnp.exp(la - mx), jnp.exp(lb - mx), jnp.exp(lc - mx)
        parts.append((wa * o1[:, cols] + wb * o2_scr[cc] + wc * o3_scr[cc]) / (wa + wb + wc))
    oa = jnp.concatenate(parts, axis=1)
    mix = jnp.dot(oa.astype(BF16), w_ref[:GROUP_WIDTH, :], preferred_element_type=F32)
    mix = mix + jnp.dot(sb_ref[...], w_ref[GROUP_WIDTH:, :], preferred_element_type=F32)
    y = _layer_norm(DEEPNORM_ALPHA * x_ref[...] + mix, g_ref[...], b_ref[...])
    y_ref[...] = y
    yb_ref[...] = y.astype(BF16)


def _outproj_ln(dil_outs, sb_out, x2, w_out_b, g, b, seq_len, tm=512):
    n_tok = x2.shape[0]
    pos_blocks = seq_len // tm
    half_spec = pl.BlockSpec((tm, GROUP_WIDTH), lambda i: (i, 0))
    full_spec = pl.BlockSpec((tm, D_MODEL), lambda i: (i, 0))
    vec_spec = pl.BlockSpec((1, D_MODEL), lambda i: (0, 0))

    def grouped_spec(dil):
        return pl.BlockSpec((None, dil, tm // dil, GROUP_WIDTH),
                            lambda i: (i // pos_blocks, 0, i % pos_blocks, 0))

    (o1, l1), (o2, l2), (o3, l3) = dil_outs
    s2, s3 = grouped_spec(o2.shape[1]), grouped_spec(o3.shape[1])
    return pl.pallas_call(
        _outproj_kernel,
        grid=(n_tok // tm,),
        in_specs=[half_spec, s2, s3, half_spec, s2, s3, half_spec, full_spec,
                  pl.BlockSpec((D_MODEL, D_MODEL), lambda i: (0, 0)),
                  vec_spec, vec_spec],
        out_specs=[full_spec, full_spec],
        out_shape=[jax.ShapeDtypeStruct((n_tok, D_MODEL), F32),
                   jax.ShapeDtypeStruct((n_tok, D_MODEL), BF16)],
        scratch_shapes=[pltpu.VMEM((GROUP_WIDTH // LANES, tm, LANES), F32)] * 4,
        compiler_params=_cparams("parallel"),
        name="outproj_ln1",
    )(o1, o2, o3, l1, l2, l3, sb_out, x2, w_out_b, g.reshape(1, D_MODEL), b.reshape(1, D_MODEL))


def _staircase():
    return [(a, b) for a in range(PEER_TOPK) for b in range(PEER_TOPK)
            if (a + 1) * (b + 1) <= PEER_TOPK]


def _peer_select_kernel(x_ref, wq_ref, keys_ref, cnt_out, e0_out, rank_out, e1_out,
                        q_scr, sc_scr, rank_scr, top_scr, cnt_scr, rz_scr, *, tm):
    n_hp = 2 * PEER_HEADS
    n_chunk = tm // LANES
    q_scr[...] = jnp.dot(x_ref[...], wq_ref[...], preferred_element_type=F32).astype(BF16)
    for hp in range(n_hp):
        sc = lax.dot_general(
            keys_ref[hp], q_scr[:, hp * N_SUBKEYS:(hp + 1) * N_SUBKEYS],
            (((1,), (1,)), ((), ())), preferred_element_type=F32)
        for c in range(n_chunk):
            sc_scr[hp, c] = sc[:, c * LANES:(c + 1) * LANES]

    key_id = lax.broadcasted_iota(jnp.int32, (N_SUBKEYS, LANES), 0)

    def stage1(idx, carry):
        head = idx // n_chunk
        c = idx % n_chunk
        hps = (2 * head, 2 * head + 1)
        s = [sc_scr[hp, c] for hp in hps]
        rank = [jnp.full((N_SUBKEYS, LANES), float(PEER_TOPK), F32) for _ in hps]
        for it in range(PEER_TOPK):
            for p, hp in enumerate(hps):
                m = jnp.max(s[p], axis=0, keepdims=True)
                first = jnp.min(jnp.where(s[p] == m, key_id, N_SUBKEYS), axis=0, keepdims=True)
                sel = key_id == first
                rank[p] = jnp.where(sel, float(it), rank[p])
                s[p] = jnp.where(sel, -jnp.inf, s[p])
                top_scr[hp, c, pl.ds(it, 1), :] = m
        for p, hp in enumerate(hps):
            rank_scr[hp, c] = rank[p]
        return carry

    lax.fori_loop(0, PEER_HEADS * n_chunk, stage1, 0, unroll=2)

    pairs = _staircase()

    def stage2(c, carry):
        def heads_on_sublanes(half, a):
            return jnp.concatenate(
                [top_scr[2 * h + half, c, pl.ds(a, 1), :] for h in range(PEER_HEADS)], axis=0)

        t0 = [heads_on_sublanes(0, a) for a in range(PEER_TOPK)]
        t1 = [heads_on_sublanes(1, b) for b in range(PEER_TOPK)]
        cand = [t0[a] + t1[b] for a, b in pairs]
        best = cand[0]
        zsum = jnp.zeros((PEER_HEADS, LANES), F32)

        def tree(op, vals):
            while len(vals) > 1:
                vals = [op(vals[i], vals[i + 1]) if i + 1 < len(vals) else vals[i]
                        for i in range(0, len(vals), 2)]
            return vals[0]

        for it in range(PEER_TOPK):
            m = tree(jnp.maximum, cand)
            zsum = zsum + jnp.exp(m - best)
            first = tree(jnp.minimum, [jnp.where(cand[k] == m, float(k), float(len(pairs)))
                                       for k in range(len(pairs))])
            cand = [jnp.where(first == float(k), -jnp.inf, cand[k]) for k in range(len(pairs))]
        for a in range(PEER_TOPK):
            taken = [jnp.where(cand[k] == -jnp.inf, 1.0, 0.0)
                     for k, (pa, _) in enumerate(pairs) if pa == a]
            cnt_scr[c, a] = tree(jnp.add, taken)
        rz_scr[c] = 0.5 / zsum
        return carry

    lax.fori_loop(0, n_chunk, stage2, 0, unroll=2)

    def stage3(c, carry):
        for head in range(PEER_HEADS):
            rank0 = rank_scr[2 * head, c]
            cntk = jnp.zeros((N_SUBKEYS, LANES), F32)
            for a in range(PEER_TOPK):
                ca = cnt_scr[c, a, pl.ds(head, 1), :]
                cntk = jnp.where(rank0 == float(a), ca, cntk)
            m0 = top_scr[2 * head, c, pl.ds(0, 1), :]
            m1 = top_scr[2 * head + 1, c, pl.ds(0, 1), :]
            rz = rz_scr[c, pl.ds(head, 1), :]
            cnt_out[head, c] = cntk
            e0_out[head, c] = jnp.exp(sc_scr[2 * head, c] - m0) * rz
            rank_out[head, c] = rank_scr[2 * head + 1, c]
            e1_out[head, c] = jnp.exp(sc_scr[2 * head + 1, c] - m1)
        return carry

    lax.fori_loop(0, n_chunk, stage3, 0)


def _peer_select(x1b, wq_b, keys_b, tm=256):
    n_tok = x1b.shape[0]
    n_hp = 2 * PEER_HEADS
    qdim = n_hp * N_SUBKEYS
    n_chunk = tm // LANES
    out_spec = pl.BlockSpec((PEER_HEADS, n_chunk, N_SUBKEYS, LANES), lambda i: (0, i, 0, 0))
    out_sds = jax.ShapeDtypeStruct((PEER_HEADS, n_tok // LANES, N_SUBKEYS, LANES), F32)
    return pl.pallas_call(
        functools.partial(_peer_select_kernel, tm=tm),
        grid=(n_tok // tm,),
        in_specs=[pl.BlockSpec((tm, D_MODEL), lambda i: (i, 0)),
                  pl.BlockSpec((D_MODEL, qdim), lambda i: (0, 0)),
                  pl.BlockSpec((n_hp, N_SUBKEYS, N_SUBKEYS), lambda i: (0, 0, 0))],
        out_specs=[out_spec] * 4,
        out_shape=[out_sds] * 4,
        scratch_shapes=[pltpu.VMEM((tm, qdim), BF16),
                        pltpu.VMEM((n_hp, n_chunk, N_SUBKEYS, LANES), F32),
                        pltpu.VMEM((n_hp, n_chunk, N_SUBKEYS, LANES), F32),
                        pltpu.VMEM((n_hp, n_chunk, PEER_TOPK, LANES), F32),
                        pltpu.VMEM((n_chunk, PEER_TOPK, PEER_HEADS, LANES), F32),
                        pltpu.VMEM((n_chunk, PEER_HEADS, LANES), F32)],
        compiler_params=_cparams("parallel"),
        name="peer_select",
    )(x1b, wq_b, keys_b)


def _peer_dense_kernel(x_ref, u_ref, vt_ref, cnt_ref, e0_ref, rank_ref, e1_ref,
                       g_ref, b_ref, o_ref, acc_ref, xt_ref, ht_ref, act_ref, *, tm, te):
    ei = pl.program_id(1)
    n_chunk = tm // LANES
    rows = te // N_SUBKEYS
    jrows = GATE_TILE_ROWS

    @pl.when(ei == 0)
    def _():
        acc_ref[...] = jnp.zeros_like(acc_ref)
        xt_ref[...] = x_ref[...].T.astype(BF16)

    ht_ref[...] = jnp.dot(u_ref[...], xt_ref[...], preferred_element_type=F32)

    for c in range(n_chunk):
        lanes = slice(c * LANES, (c + 1) * LANES)

        def gate_body(jb, carry, c=c, lanes=lanes):
            j0 = pl.multiple_of(jb * jrows, jrows)
            for r_lo in range(0, rows, GATE_TILE_KEYS):
                r_tile = range(r_lo, r_lo + GATE_TILE_KEYS)
                gates = {r: jnp.zeros((jrows, LANES), F32) for r in r_tile}
                for h in range(PEER_HEADS):
                    rank1 = rank_ref[h, c, pl.ds(j0, jrows), :]
                    e1 = e1_ref[h, c, pl.ds(j0, jrows), :]
                    for r in r_tile:
                        cnt = cnt_ref[h, c, pl.ds(r, 1), :]
                        e0 = e0_ref[h, c, pl.ds(r, 1), :]
                        gates[r] = gates[r] + jnp.where(rank1 < cnt, e1 * e0, 0.0)
                for r in r_tile:
                    row0 = pl.multiple_of(r * N_SUBKEYS + j0, jrows)
                    hh = ht_ref[pl.ds(row0, jrows), lanes]
                    gelu2 = hh * (1.0 + lax.erf(hh * (2.0 ** -0.5)))
                    act_ref[pl.ds(row0, jrows), lanes] = (gates[r] * gelu2).astype(BF16)
            return carry

        lax.fori_loop(0, N_SUBKEYS // jrows, gate_body, 0)

    acc_ref[...] += jnp.dot(vt_ref[...], act_ref[...],
                            preferred_element_type=F32)

    @pl.when(ei == pl.num_programs(1) - 1)
    def _():
        ffn = acc_ref[...].T
        o_ref[...] = _layer_norm(DEEPNORM_ALPHA * x_ref[...] + ffn, g_ref[...], b_ref[...])


def _peer_dense(x1, u_b, vt_b, sel, g, b, tm=512, te=2048):
    n_tok = x1.shape[0]
    n_chunk = tm // LANES
    sel_spec = pl.BlockSpec((PEER_HEADS, n_chunk, N_SUBKEYS, LANES), lambda i, e: (0, i, 0, 0))
    row_spec = pl.BlockSpec((PEER_HEADS, n_chunk, te // N_SUBKEYS, LANES),
                            lambda i, e: (0, i, e, 0))
    vec_spec = pl.BlockSpec((1, D_MODEL), lambda i, e: (0, 0))
    return pl.pallas_call(
        functools.partial(_peer_dense_kernel, tm=tm, te=te),
        grid=(n_tok // tm, N_EXPERTS // te),
        in_specs=[pl.BlockSpec((tm, D_MODEL), lambda i, e: (i, 0)),
                  pl.BlockSpec((te, D_MODEL), lambda i, e: (e, 0)),
                  pl.BlockSpec((D_MODEL, te), lambda i, e: (0, e)),
                  row_spec, row_spec, sel_spec, sel_spec, vec_spec, vec_spec],
        out_specs=pl.BlockSpec((tm, D_MODEL), lambda i, e: (i, 0)),
        out_shape=jax.ShapeDtypeStruct((n_tok, D_MODEL), F32),
        scratch_shapes=[pltpu.VMEM((D_MODEL, tm), F32), pltpu.VMEM((D_MODEL, tm), BF16),
                        pltpu.VMEM((te, tm), F32), pltpu.VMEM((te, tm), BF16)],
        compiler_params=_cparams("parallel", "arbitrary"),
        name="peer_dense",
    )(x1, u_b, vt_b, *sel, g.reshape(1, D_MODEL), b.reshape(1, D_MODEL))


def _layer(x, w_in, w_out, ln1_g, ln1_b, peer_wq, sub_keys, peer_u, peer_v, ln2_g, ln2_b):
    batch, seq_len, _ = x.shape
    n_tok = batch * seq_len
    x2 = x.reshape(n_tok, D_MODEL)

    qkv, qkv_r4, qkv_r16 = _qkv_proj(x2, w_in.astype(BF16), batch, seq_len)
    qkv3 = qkv.reshape(batch, seq_len, QKV_WIDTH)
    o1, l1 = _dilated_branch(qkv3.reshape(batch, 1, seq_len, QKV_WIDTH), DILATIONS[0])
    dil_outs = [(o1.reshape(n_tok, GROUP_WIDTH), l1.reshape(n_tok, GROUP_WIDTH)),
                _dilated_branch(qkv_r4, DILATIONS[1]),
                _dilated_branch(qkv_r16, DILATIONS[2])]
    sb_out = _stick_breaking(qkv3)
    x1, x1b = _outproj_ln(dil_outs, sb_out, x2, w_out.astype(BF16), ln1_g, ln1_b, seq_len)

    keys_b = sub_keys.reshape(2 * PEER_HEADS, N_SUBKEYS, N_SUBKEYS).astype(BF16)
    sel = _peer_select(x1b, peer_wq.astype(BF16), keys_b)
    out = _peer_dense(x1, peer_u.astype(BF16), peer_v.astype(BF16).T, sel, ln2_g, ln2_b)
    return out.reshape(batch, seq_len, D_MODEL)


def kernel(x, w_in, w_out, ln1_g, ln1_b, peer_wq, peer_sub_keys, peer_u, peer_v, ln2_g, ln2_b):
    depth = w_in.shape[0]
    for layer in range(depth):
        x = _layer(x, w_in[layer], w_out[layer], ln1_g[layer], ln1_b[layer],
                   peer_wq[layer], peer_sub_keys[layer], peer_u[layer], peer_v[layer],
                   ln2_g[layer], ln2_b[layer])
    return x
```

```python
import functools

import jax
import jax.numpy as jnp
from jax import lax
from jax.experimental import pallas as pl
from jax.experimental.pallas import tpu as pltpu

F32 = jnp.float32
BF16 = jnp.bfloat16

D_MODEL = 1024
HEAD_DIM = 64
GROUP_WIDTH = 512
QKV_WIDTH = 6 * GROUP_WIDTH
DILATIONS = (1, 4, 16)
WINDOW_STEPS = 128
ROPE_DIM = HEAD_DIM // 4
ROPE_THETA = 500000.0
PEER_HEADS = 8
N_SUBKEYS = 128
N_EXPERTS = N_SUBKEYS * N_SUBKEYS
PEER_TOPK = 16
LN_EPS = 1e-5
DEEPNORM_ALPHA = 2.0 ** 0.25
LANES = 128
MXU_COLS = 256
NEG_BIG = -1e30
GATE_TILE_KEYS = 8
GATE_TILE_ROWS = 32
SB_EXP_UNDERFLOW = -104.0

VMEM_LIMIT = 56 * 1024 * 1024


def _cparams(*sem):
    return pltpu.CompilerParams(dimension_semantics=sem, vmem_limit_bytes=VMEM_LIMIT)


def _rope_tables(seq_len):
    half = ROPE_DIM // 2
    inv_freq = ROPE_THETA ** (-jnp.arange(half, dtype=F32) * 2.0 / ROPE_DIM)
    ang = jnp.arange(seq_len, dtype=F32)[:, None] * inv_freq[None, :]
    cos, sin = jnp.cos(ang), jnp.sin(ang)
    rest = HEAD_DIM - ROPE_DIM
    ones = jnp.ones((seq_len, rest), F32)
    zeros = jnp.zeros((seq_len, rest), F32)
    zh = jnp.zeros((seq_len, half), F32)
    cos_t = jnp.concatenate([cos, cos, ones], axis=1)
    up_t = jnp.concatenate([-sin, zh, zeros], axis=1)
    dn_t = jnp.concatenate([zh, sin, zeros], axis=1)
    rep = LANES // HEAD_DIM
    return tuple(jnp.tile(t, (1, rep)) for t in (cos_t, up_t, dn_t))


def _qkv_kernel(x_ref, w_ref, cos_ref, up_ref, dn_ref, o_ref, o4_ref, o16_ref, stage_ref):
    tm = x_ref.shape[0]
    xb = x_ref[...].astype(BF16)
    half = ROPE_DIM // 2
    for j in range(6):
        acc = jnp.dot(xb, w_ref[:, j * GROUP_WIDTH:(j + 1) * GROUP_WIDTH],
                      preferred_element_type=F32)
        for c in range(GROUP_WIDTH // LANES):
            a = acc[:, c * LANES:(c + 1) * LANES]
            if j in (0, 1):
                up = pltpu.roll(a, LANES - half, axis=1)
                dn = pltpu.roll(a, half, axis=1)
                a = a * cos_ref[...] + up * up_ref[...] + dn * dn_ref[...]
            if j in (0, 3):
                a = a * (HEAD_DIM ** -0.5)
            col = j * GROUP_WIDTH + c * LANES
            o_ref[:, col:col + LANES] = a.astype(BF16)
            if j < 3:
                stage_ref[col // LANES] = a
    for dil, od_ref in ((4, o4_ref), (16, o16_ref)):
        for r in range(dil):
            for cc in range(stage_ref.shape[0]):
                od_ref[r, :, cc * LANES:(cc + 1) * LANES] = (
                    stage_ref[cc, pl.ds(r, tm // dil, stride=dil), :].astype(BF16))


def _qkv_proj(x2, w_in_b, batch, seq_len, tm=512):
    n_tok = x2.shape[0]
    cos_t, up_t, dn_t = _rope_tables(seq_len)
    pos_blocks = seq_len // tm
    tab_spec = pl.BlockSpec((tm, LANES), lambda i: (i % pos_blocks, 0))
    a_width = 3 * GROUP_WIDTH

    def regrouped(dil):
        spec = pl.BlockSpec((None, dil, tm // dil, a_width),
                            lambda i: (i // pos_blocks, 0, i % pos_blocks, 0))
        sds = jax.ShapeDtypeStruct((batch, dil, seq_len // dil, a_width), BF16)
        return spec, sds

    spec4, sds4 = regrouped(4)
    spec16, sds16 = regrouped(16)
    return pl.pallas_call(
        _qkv_kernel,
        grid=(n_tok // tm,),
        in_specs=[pl.BlockSpec((tm, D_MODEL), lambda i: (i, 0)),
                  pl.BlockSpec((D_MODEL, QKV_WIDTH), lambda i: (0, 0)),
                  tab_spec, tab_spec, tab_spec],
        out_specs=[pl.BlockSpec((tm, QKV_WIDTH), lambda i: (i, 0)), spec4, spec16],
        out_shape=[jax.ShapeDtypeStruct((n_tok, QKV_WIDTH), BF16), sds4, sds16],
        scratch_shapes=[pltpu.VMEM((a_width // LANES, tm, LANES), F32)],
        compiler_params=_cparams("parallel"),
        name="qkv_proj",
    )(x2, w_in_b, cos_t, up_t, dn_t)


def _dil_kernel(q_ref, k_ref, v_ref, o_ref, lse_ref, *, length):
    tq = LANES
    tk = 2 * LANES
    lane = lax.broadcasted_iota(jnp.int32, (1, LANES), 1)
    head0 = lane < HEAD_DIM
    row = lax.broadcasted_iota(jnp.int32, (tq, tk), 0)
    col = lax.broadcasted_iota(jnp.int32, (tq, tk), 1)
    rel = row - col

    n_blocks = length // tq
    n_res = q_ref.shape[0]

    def body(idx, carry):
        res = idx // n_blocks
        q0 = pl.multiple_of((idx % n_blocks) * tq, tq)
        k0 = pl.multiple_of(jnp.maximum(q0 - tq, 0), tq)
        q = q_ref[res, pl.ds(q0, tq), :]
        k = k_ref[res, pl.ds(k0, tk), :]
        v = v_ref[res, pl.ds(k0, tk), :]
        diff = rel + (q0 - k0)
        valid = jnp.where(diff >= 0, diff, WINDOW_STEPS + 1) <= WINDOW_STEPS
        outs, lses = [], []
        for h in range(2):
            hm = head0 if h == 0 else jnp.logical_not(head0)
            qh = jnp.where(hm, q, jnp.zeros_like(q))
            s = lax.dot_general(qh, k, (((1,), (1,)), ((), ())),
                                preferred_element_type=F32)
            s = jnp.where(valid, s, NEG_BIG)
            m = jnp.max(s, axis=1, keepdims=True)
            p = jnp.exp(s - m)
            l = jnp.sum(p, axis=1, keepdims=True)
            oh = jnp.dot(p.astype(BF16), v, preferred_element_type=F32)
            outs.append(oh / l)
            lses.append(m + jnp.log(l))
        o_ref[res, pl.ds(q0, tq), :] = jnp.where(head0, outs[0], outs[1])
        lse_ref[res, pl.ds(q0, tq), :] = jnp.where(head0, lses[0], lses[1])
        return carry

    lax.fori_loop(0, n_res * n_blocks, body, 0, unroll=8)


def _dilated_branch(qkv_regrouped, dil):
    batch, _, length, _ = qkv_regrouped.shape
    gcols = GROUP_WIDTH // LANES
    n_res = min(dil, max(1, (8 * LANES) // length))

    def in_spec(group):
        return pl.BlockSpec((None, n_res, length, LANES),
                            lambda b, r, hp: (b, r, 0, group * gcols + hp))

    out_spec = pl.BlockSpec((None, n_res, length, LANES), lambda b, r, hp: (b, r, 0, hp))
    out_sds = jax.ShapeDtypeStruct((batch, dil, length, GROUP_WIDTH), F32)
    return pl.pallas_call(
        functools.partial(_dil_kernel, length=length),
        grid=(batch, dil // n_res, gcols),
        in_specs=[in_spec(0), in_spec(1), in_spec(2)],
        out_specs=[out_spec, out_spec],
        out_shape=[out_sds, out_sds],
        compiler_params=_cparams("parallel", "parallel", "parallel"),
        name=f"dilated_attn_d{dil}",
    )(qkv_regrouped, qkv_regrouped, qkv_regrouped)


def _sb_kernel(q_ref, k_ref, v_ref, o_ref, acc_ref, run_ref, *, seq_len, blk):
    lane = lax.broadcasted_iota(jnp.int32, (1, LANES), 1)
    head0 = lane < HEAD_DIM
    row = lax.broadcasted_iota(jnp.int32, (blk, blk), 0)
    col = lax.broadcasted_iota(jnp.int32, (blk, blk), 1)
    below = row > col
    tri = jnp.where(below, 1.0, 0.0).astype(BF16)
    tri2 = jnp.concatenate([tri, tri], axis=0)
    reps = blk // LANES

    def q_body(qi, carry):
        q0 = pl.multiple_of(qi * blk, blk)
        q = q_ref[pl.ds(q0, blk), :]
        qh = [jnp.where(head0, q, jnp.zeros_like(q)), jnp.where(head0, jnp.zeros_like(q), q)]
        acc_ref[...] = jnp.zeros_like(acc_ref)
        run_ref[...] = jnp.zeros_like(run_ref)

        def blocks(sweep):
            kv = []
            for kj, diag in sweep:
                k0 = pl.multiple_of(kj * blk, blk)
                kv.append((k_ref[pl.ds(k0, blk), :], v_ref[pl.ds(k0, blk), :], diag))
            tops = []
            for h in range(2):
                parts = []
                for k, v, diag in kv:
                    z = lax.dot_general(qh[h], k, (((1,), (1,)), ((), ())),
                                        preferred_element_type=F32)
                    sp = jnp.maximum(z, 0.0) + jnp.log1p(jnp.exp(-jnp.abs(z)))
                    log_1m = -sp
                    log_b = z - sp
                    if diag:
                        log_1m = jnp.where(below, log_1m, 0.0)
                    hi = log_1m.astype(BF16)
                    lo = (log_1m - hi.astype(F32)).astype(BF16)
                    later = jnp.dot(jnp.concatenate([hi, lo], axis=1), tri2,
                                    preferred_element_type=F32)
                    parts.append((log_b + later, jnp.sum(log_1m, axis=1, keepdims=True), v, diag))
                run = run_ref[h]
                for pre, row_sum, v, diag in parts:
                    run_w = jnp.concatenate([run] * reps, axis=1)
                    a = jnp.exp(pre + run_w)
                    if diag:
                        a = jnp.where(below, a, 0.0)
                    acc_ref[h] += jnp.dot(a.astype(BF16), v, preferred_element_type=F32)
                    run = run + row_sum
                run_ref[h] = run
                tops.append(jnp.max(run))
            return jnp.maximum(tops[0], tops[1])

        top = lax.cond(qi > 0,
                       lambda: blocks([(qi, True), (qi - 1, False)]),
                       lambda: blocks([(qi, True)]))

        def k_cond(c):
            return jnp.logical_and(c[0] >= 0, c[1] >= SB_EXP_UNDERFLOW)

        def k_body(c):
            return c[0] - 1, blocks([(c[0], False)])

        lax.while_loop(k_cond, k_body, (qi - 2, top))
        o_ref[pl.ds(q0, blk), :] = jnp.where(head0, acc_ref[0], acc_ref[1]).astype(o_ref.dtype)
        return carry

    lax.fori_loop(0, seq_len // blk, q_body, 0)


def _stick_breaking(qkv3, blk=256):
    batch, seq_len, _ = qkv3.shape
    gcols = GROUP_WIDTH // LANES

    def in_spec(group):
        return pl.BlockSpec((None, seq_len, LANES),
                            lambda b, hp: (b, 0, (3 + group) * gcols + hp))

    out = pl.pallas_call(
        functools.partial(_sb_kernel, seq_len=seq_len, blk=blk),
        grid=(batch, gcols),
        in_specs=[in_spec(0), in_spec(1), in_spec(2)],
        out_specs=pl.BlockSpec((None, seq_len, LANES), lambda b, hp: (b, 0, hp)),
        out_shape=jax.ShapeDtypeStruct((batch, seq_len, GROUP_WIDTH), BF16),
        scratch_shapes=[pltpu.VMEM((2, blk, LANES), F32), pltpu.VMEM((2, blk, LANES), F32)],
        compiler_params=_cparams("parallel", "parallel"),
        name="stick_breaking_attn",
    )(qkv3, qkv3, qkv3)
    return out.reshape(batch * seq_len, GROUP_WIDTH)


def _layer_norm(y, g, b):
    mu = jnp.mean(y, axis=-1, keepdims=True)
    yc = y - mu
    var = jnp.mean(yc * yc, axis=-1, keepdims=True)
    return yc * lax.rsqrt(var + LN_EPS) * g + b


def _outproj_kernel(o1, o2, o3, l1, l2, l3, sb_ref, x_ref, w_ref, g_ref, b_ref,
                    y_ref, yb_ref, o2_scr, o3_scr, l2_scr, l3_scr):
    tm = x_ref.shape[0]
    n_col = GROUP_WIDTH // LANES
    for src, dst in ((o2, o2_scr), (l2, l2_scr), (o3, o3_scr), (l3, l3_scr)):
        dil = src.shape[0]
        for r in range(dil):
            for cc in range(n_col):
                dst[cc, pl.ds(r, tm // dil, stride=dil), :] = src[r, :, cc * LANES:(cc + 1) * LANES]
    parts = []
    for cc in range(n_col):
        cols = slice(cc * LANES, (cc + 1) * LANES)
        la, lb, lc = l1[:, cols], l2_scr[cc], l3_scr[cc]
        mx = jnp.maximum(jnp.maximum(la, lb), lc)
        wa, wb, wc = jnp.exp(la - mx), jnp.exp(lb - mx), jnp.exp(lc - mx)
        parts.append((wa * o1[:, cols] + wb * o2_scr[cc] + wc * o3_scr[cc]) / (wa + wb + wc))
    oa = jnp.concatenate(parts, axis=1)
    mix = jnp.dot(oa.astype(BF16), w_ref[:GROUP_WIDTH, :], preferred_element_type=F32)
    mix = mix + jnp.dot(sb_ref[...], w_ref[GROUP_WIDTH:, :], preferred_element_type=F32)
    y = _layer_norm(DEEPNORM_ALPHA * x_ref[...] + mix, g_ref[...], b_ref[...])
    y_ref[...] = y
    yb_ref[...] = y.astype(BF16)


def _outproj_ln(dil_outs, sb_out, x2, w_out_b, g, b, seq_len, tm=512):
    n_tok = x2.shape[0]
    pos_blocks = seq_len // tm
    half_spec = pl.BlockSpec((tm, GROUP_WIDTH), lambda i: (i, 0))
    full_spec = pl.BlockSpec((tm, D_MODEL), lambda i: (i, 0))
    vec_spec = pl.BlockSpec((1, D_MODEL), lambda i: (0, 0))

    def grouped_spec(dil):
        return pl.BlockSpec((None, dil, tm // dil, GROUP_WIDTH),
                            lambda i: (i // pos_blocks, 0, i % pos_blocks, 0))

    (o1, l1), (o2, l2), (o3, l3) = dil_outs
    s2, s3 = grouped_spec(o2.shape[1]), grouped_spec(o3.shape[1])
    return pl.pallas_call(
        _outproj_kernel,
        grid=(n_tok // tm,),
        in_specs=[half_spec, s2, s3, half_spec, s2, s3, half_spec, full_spec,
                  pl.BlockSpec((D_MODEL, D_MODEL), lambda i: (0, 0)),
                  vec_spec, vec_spec],
        out_specs=[full_spec, full_spec],
        out_shape=[jax.ShapeDtypeStruct((n_tok, D_MODEL), F32),
                   jax.ShapeDtypeStruct((n_tok, D_MODEL), BF16)],
        scratch_shapes=[pltpu.VMEM((GROUP_WIDTH // LANES, tm, LANES), F32)] * 4,
        compiler_params=_cparams("parallel"),
        name="outproj_ln1",
    )(o1, o2, o3, l1, l2, l3, sb_out, x2, w_out_b, g.reshape(1, D_MODEL), b.reshape(1, D_MODEL))


def _staircase():
    return [(a, b) for a in range(PEER_TOPK) for b in range(PEER_TOPK)
            if (a + 1) * (b + 1) <= PEER_TOPK]


def _peer_select_kernel(x_ref, wq_ref, keys_ref, cnt_out, e0_out, rank_out, e1_out,
                        q_scr, sc_scr, rank_scr, top_scr, cnt_scr, rz_scr, *, tm):
    n_hp = 2 * PEER_HEADS
    n_chunk = tm // LANES
    q_scr[...] = jnp.dot(x_ref[...], wq_ref[...], preferred_element_type=F32).astype(BF16)
    for hp in range(n_hp):
        sc = lax.dot_general(
            keys_ref[hp], q_scr[:, hp * N_SUBKEYS:(hp + 1) * N_SUBKEYS],
            (((1,), (1,)), ((), ())), preferred_element_type=F32)
        for c in range(n_chunk):
            sc_scr[hp, c] = sc[:, c * LANES:(c + 1) * LANES]

    key_id = lax.broadcasted_iota(jnp.int32, (N_SUBKEYS, LANES), 0)

    def stage1(idx, carry):
        head = idx // n_chunk
        c = idx % n_chunk
        hps = (2 * head, 2 * head + 1)
        s = [sc_scr[hp, c] for hp in hps]
        rank = [jnp.full((N_SUBKEYS, LANES), float(PEER_TOPK), F32) for _ in hps]
        for it in range(PEER_TOPK):
            for p, hp in enumerate(hps):
                m = jnp.max(s[p], axis=0, keepdims=True)
                first = jnp.min(jnp.where(s[p] == m, key_id, N_SUBKEYS), axis=0, keepdims=True)
                sel = key_id == first
                rank[p] = jnp.where(sel, float(it), rank[p])
                s[p] = jnp.where(sel, -jnp.inf, s[p])
                top_scr[hp, c, pl.ds(it, 1), :] = m
        for p, hp in enumerate(hps):
            rank_scr[hp, c] = rank[p]
        return carry

    lax.fori_loop(0, PEER_HEADS * n_chunk, stage1, 0, unroll=2)

    pairs = _staircase()

    def stage2(c, carry):
        def heads_on_sublanes(half, a):
            return jnp.concatenate(
                [top_scr[2 * h + half, c, pl.ds(a, 1), :] for h in range(PEER_HEADS)], axis=0)

        t0 = [heads_on_sublanes(0, a) for a in range(PEER_TOPK)]
        t1 = [heads_on_sublanes(1, b) for b in range(PEER_TOPK)]
        cand = [t0[a] + t1[b] for a, b in pairs]
        best = cand[0]
        zsum = jnp.zeros((PEER_HEADS, LANES), F32)

        def tree(op, vals):
            while len(vals) > 1:
                vals = [op(vals[i], vals[i + 1]) if i + 1 < len(vals) else vals[i]
                        for i in range(0, len(vals), 2)]
            return vals[0]

        for it in range(PEER_TOPK):
            m = tree(jnp.maximum, cand)
            zsum = zsum + jnp.exp(m - best)
            first = tree(jnp.minimum, [jnp.where(cand[k] == m, float(k), float(len(pairs)))
                                       for k in range(len(pairs))])
            cand = [jnp.where(first == float(k), -jnp.inf, cand[k]) for k in range(len(pairs))]
        for a in range(PEER_TOPK):
            taken = [jnp.where(cand[k] == -jnp.inf, 1.0, 0.0)
                     for k, (pa, _) in enumerate(pairs) if pa == a]
            cnt_scr[c, a] = tree(jnp.add, taken)
        rz_scr[c] = 0.5 / zsum
        return carry

    lax.fori_loop(0, n_chunk, stage2, 0, unroll=2)

    def stage3(c, carry):
        for head in range(PEER_HEADS):
            rank0 = rank_scr[2 * head, c]
            cntk = jnp.zeros((N_SUBKEYS, LANES), F32)
            for a in range(PEER_TOPK):
                ca = cnt_scr[c, a, pl.ds(head, 1), :]
                cntk = jnp.where(rank0 == float(a), ca, cntk)
            m0 = top_scr[2 * head, c, pl.ds(0, 1), :]
            m1 = top_scr[2 * head + 1, c, pl.ds(0, 1), :]
            rz = rz_scr[c, pl.ds(head, 1), :]
            cnt_out[head, c] = cntk
            e0_out[head, c] = jnp.exp(sc_scr[2 * head, c] - m0) * rz
            rank_out[head, c] = rank_scr[2 * head + 1, c]
            e1_out[head, c] = jnp.exp(sc_scr[2 * head + 1, c] - m1)
        return carry

    lax.fori_loop(0, n_chunk, stage3, 0)


def _peer_select(x1b, wq_b, keys_b, tm=256):
    n_tok = x1b.shape[0]
    n_hp = 2 * PEER_HEADS
    qdim = n_hp * N_SUBKEYS
    n_chunk = tm // LANES
    out_spec = pl.BlockSpec((PEER_HEADS, n_chunk, N_SUBKEYS, LANES), lambda i: (0, i, 0, 0))
    out_sds = jax.ShapeDtypeStruct((PEER_HEADS, n_tok // LANES, N_SUBKEYS, LANES), F32)
    return pl.pallas_call(
        functools.partial(_peer_select_kernel, tm=tm),
        grid=(n_tok // tm,),
        in_specs=[pl.BlockSpec((tm, D_MODEL), lambda i: (i, 0)),
                  pl.BlockSpec((D_MODEL, qdim), lambda i: (0, 0)),
                  pl.BlockSpec((n_hp, N_SUBKEYS, N_SUBKEYS), lambda i: (0, 0, 0))],
        out_specs=[out_spec] * 4,
        out_shape=[out_sds] * 4,
        scratch_shapes=[pltpu.VMEM((tm, qdim), BF16),
                        pltpu.VMEM((n_hp, n_chunk, N_SUBKEYS, LANES), F32),
                        pltpu.VMEM((n_hp, n_chunk, N_SUBKEYS, LANES), F32),
                        pltpu.VMEM((n_hp, n_chunk, PEER_TOPK, LANES), F32),
                        pltpu.VMEM((n_chunk, PEER_TOPK, PEER_HEADS, LANES), F32),
                        pltpu.VMEM((n_chunk, PEER_HEADS, LANES), F32)],
        compiler_params=_cparams("parallel"),
        name="peer_select",
    )(x1b, wq_b, keys_b)


def _peer_dense_kernel(x_ref, u_ref, vt_ref, cnt_ref, e0_ref, rank_ref, e1_ref,
                       g_ref, b_ref, o_ref, acc_ref, xt_ref, ht_ref, act_ref, *, tm, te):
    ei = pl.program_id(1)
    n_chunk = tm // LANES
    rows = te // N_SUBKEYS
    jrows = GATE_TILE_ROWS

    @pl.when(ei == 0)
    def _():
        acc_ref[...] = jnp.zeros_like(acc_ref)
        xt_ref[...] = x_ref[...].T.astype(BF16)

    ht_ref[...] = jnp.dot(u_ref[...], xt_ref[...], preferred_element_type=F32)

    for c in range(n_chunk):
        lanes = slice(c * LANES, (c + 1) * LANES)

        def gate_body(jb, carry, c=c, lanes=lanes):
            j0 = pl.multiple_of(jb * jrows, jrows)
            for r_lo in range(0, rows, GATE_TILE_KEYS):
                r_tile = range(r_lo, r_lo + GATE_TILE_KEYS)
                gates = {r: jnp.zeros((jrows, LANES), F32) for r in r_tile}
                for h in range(PEER_HEADS):
                    rank1 = rank_ref[h, c, pl.ds(j0, jrows), :]
                    e1 = e1_ref[h, c, pl.ds(j0, jrows), :]
                    for r in r_tile:
                        cnt = cnt_ref[h, c, pl.ds(r, 1), :]
                        e0 = e0_ref[h, c, pl.ds(r, 1), :]
                        gates[r] = gates[r] + jnp.where(rank1 < cnt, e1 * e0, 0.0)
                for r in r_tile:
                    row0 = pl.multiple_of(r * N_SUBKEYS + j0, jrows)
                    hh = ht_ref[pl.ds(row0, jrows), lanes]
                    gelu2 = hh * (1.0 + lax.erf(hh * (2.0 ** -0.5)))
                    act_ref[pl.ds(row0, jrows), lanes] = (gates[r] * gelu2).astype(BF16)
            return carry

        lax.fori_loop(0, N_SUBKEYS // jrows, gate_body, 0)

    acc_ref[...] += jnp.dot(vt_ref[...], act_ref[...],
                            preferred_element_type=F32)

    @pl.when(ei == pl.num_programs(1) - 1)
    def _():
        ffn = acc_ref[...].T
        o_ref[...] = _layer_norm(DEEPNORM_ALPHA * x_ref[...] + ffn, g_ref[...], b_ref[...])


def _peer_dense(x1, u_b, vt_b, sel, g, b, tm=512, te=2048):
    n_tok = x1.shape[0]
    n_chunk = tm // LANES
    sel_spec = pl.BlockSpec((PEER_HEADS, n_chunk, N_SUBKEYS, LANES), lambda i, e: (0, i, 0, 0))
    row_spec = pl.BlockSpec((PEER_HEADS, n_chunk, te // N_SUBKEYS, LANES),
                            lambda i, e: (0, i, e, 0))
    vec_spec = pl.BlockSpec((1, D_MODEL), lambda i, e: (0, 0))
    return pl.pallas_call(
        functools.partial(_peer_dense_kernel, tm=tm, te=te),
        grid=(n_tok // tm, N_EXPERTS // te),
        in_specs=[pl.BlockSpec((tm, D_MODEL), lambda i, e: (i, 0)),
                  pl.BlockSpec((te, D_MODEL), lambda i, e: (e, 0)),
                  pl.BlockSpec((D_MODEL, te), lambda i, e: (0, e)),
                  row_spec, row_spec, sel_spec, sel_spec, vec_spec, vec_spec],
        out_specs=pl.BlockSpec((tm, D_MODEL), lambda i, e: (i, 0)),
        out_shape=jax.ShapeDtypeStruct((n_tok, D_MODEL), F32),
        scratch_shapes=[pltpu.VMEM((D_MODEL, tm), F32), pltpu.VMEM((D_MODEL, tm), BF16),
                        pltpu.VMEM((te, tm), F32), pltpu.VMEM((te, tm), BF16)],
        compiler_params=_cparams("parallel", "arbitrary"),
        name="peer_dense",
    )(x1, u_b, vt_b, *sel, g.reshape(1, D_MODEL), b.reshape(1, D_MODEL))


def _layer(x, w_in, w_out, ln1_g, ln1_b, peer_wq, sub_keys, peer_u, peer_v, ln2_g, ln2_b):
    batch, seq_len, _ = x.shape
    n_tok = batch * seq_len
    x2 = x.reshape(n_tok, D_MODEL)

    qkv, qkv_r4, qkv_r16 = _qkv_proj(x2, w_in.astype(BF16), batch, seq_len)
    qkv3 = qkv.reshape(batch, seq_len, QKV_WIDTH)
    o1, l1 = _dilated_branch(qkv3.reshape(batch, 1, seq_len, QKV_WIDTH), DILATIONS[0])
    dil_outs = [(o1.reshape(n_tok, GROUP_WIDTH), l1.reshape(n_tok, GROUP_WIDTH)),
                _dilated_branch(qkv_r4, DILATIONS[1]),
                _dilated_branch(qkv_r16, DILATIONS[2])]
    sb_out = _stick_breaking(qkv3)
    x1, x1b = _outproj_ln(dil_outs, sb_out, x2, w_out.astype(BF16), ln1_g, ln1_b, seq_len)

    keys_b = sub_keys.reshape(2 * PEER_HEADS, N_SUBKEYS, N_SUBKEYS).astype(BF16)
    sel = _peer_select(x1b, peer_wq.astype(BF16), keys_b)
    out = _peer_dense(x1, peer_u.astype(BF16), peer_v.astype(BF16).T, sel, ln2_g, ln2_b)
    return out.reshape(batch, seq_len, D_MODEL)


def kernel(x, w_in, w_out, ln1_g, ln1_b, peer_wq, peer_sub_keys, peer_u, peer_v, ln2_g, ln2_b):
    depth = w_in.shape[0]
    for layer in range(depth):
        x = _layer(x, w_in[layer], w_out[layer], ln1_g[layer], ln1_b[layer],
                   peer_wq[layer], peer_sub_keys[layer], peer_u[layer], peer_v[layer],
                   ln2_g[layer], ln2_b[layer])
    return x
```
